```python
import math
import jax
import jax.numpy as jnp
from jax import lax
import numpy as np

D_MODEL = 1024
BATCH = 8
SEQ = 2048
DEPTH = 4
DEC_BATCH = 128
DEC_SEQ = 4
PAST_LEN = 16384
PAGE_SIZE = 128

N_MEM = 256
RW_HEADS = 12
RW_HEAD = 64
D_RW = RW_HEADS * RW_HEAD
W_LORA = 64
A_LORA = 64
V_LORA = 32
G_LORA = 128
RW_COLS = 3 * D_RW + W_LORA + A_LORA + G_LORA
GN_EPS = 64e-5
LRU_BLOCKS = 12
LRU_BW = 64
D_LRU = LRU_BLOCKS * LRU_BW
CONV_W = 4
LRU_C = 8.0
XA_HEADS = 4
XA_HEAD = 128
D_XA = XA_HEADS * XA_HEAD
N_BRANCH = 3
D_IN = RW_COLS + 2 * D_LRU + D_XA + N_BRANCH * D_MODEL
IN_SPLITS = (RW_COLS, RW_COLS + D_LRU, RW_COLS + 2 * D_LRU, RW_COLS + 2 * D_LRU + D_XA)
RW_SPLITS = (D_RW, 2 * D_RW, 3 * D_RW, 3 * D_RW + W_LORA, 3 * D_RW + W_LORA + A_LORA)
D_FF = ((-(-8 * D_MODEL // 3) + 255) // 256) * 256
ALPHA = (2 * DEPTH) ** 0.25
BETA = (8 * DEPTH) ** -0.25
LN_EPS = 1e-5

kernel_name = 'hybrid_rwkv7_rglru_memxattn_deepnorm_step'

f32 = jnp.float32


def _layer_norm(x, g, b, eps=LN_EPS):
    xf = x.astype(f32)
    mu = jnp.mean(xf, axis=-1, keepdims=True)
    var = jnp.mean(jnp.square(xf - mu), axis=-1, keepdims=True)
    return ((xf - mu) * lax.rsqrt(var + eps) * g.astype(f32) + b.astype(f32)).astype(x.dtype)


def _token_shift(p, prev, mu):
    p_prev = jnp.concatenate([prev[:, None].astype(p.dtype), p[:, :-1]], axis=1)
    return p + (p_prev - p) * mu


def _wkv7_scan(r, decay, k, v, kk, a, s0):
    def step(S, inp):
        r_t, w_t, k_t, v_t, kk_t, a_t = inp
        sa = jnp.einsum('bhvk,bhk->bhv', S, -kk_t)
        S = (S * w_t[:, :, None, :] + sa[..., None] * (kk_t * a_t)[:, :, None, :]
             + v_t[..., None] * k_t[:, :, None, :])
        y = jnp.einsum('bhvk,bhk->bhv', S, r_t)
        return S, y
    xs = tuple(jnp.moveaxis(t.astype(f32), 1, 0) for t in (r, decay, k, v, kk, a))
    S, ys = lax.scan(step, s0.astype(f32), xs)
    return jnp.moveaxis(ys, 0, 1), S


def _rwkv7_branch(l, p_rw, shift_prev, s0, v_first, P):
    B, T, _ = p_rw.shape
    xs = _token_shift(p_rw, shift_prev, P['rw_mu'][l])
    r, k, v, wd, ad, gd = jnp.split(xs, RW_SPLITS, axis=-1)
    w = -jax.nn.softplus(-(P['rw_w0'][l] + jnp.tanh(wd) @ P['rw_w2'][l])) - 0.5
    decay = jnp.exp(-jnp.exp(w.astype(f32)))
    a = jax.nn.sigmoid(P['rw_a0'][l] + ad @ P['rw_a2'][l])
    g = jax.nn.sigmoid(gd) @ P['rw_g2'][l]
    if l == 0:
        v_first = v
    else:
        j = l - 1
        v = v + (v_first - v) * jax.nn.sigmoid(P['rw_v0'][j] + (v @ P['rw_v1'][j]) @ P['rw_v2'][j])
    heads = lambda t: t.reshape(B, T, RW_HEADS, RW_HEAD)
    kk = heads(k * P['rw_kk'][l]).astype(f32)
    kk = kk / jnp.maximum(jnp.sqrt(jnp.sum(kk * kk, axis=-1, keepdims=True)), 1e-12)
    k = k * (1 + (a - 1) * P['rw_ka'][l])
    rh, kh, vh = heads(r), heads(k), heads(v)
    y, s_new = _wkv7_scan(rh, heads(decay), kh, vh, kk, heads(a), s0)
    y = _layer_norm(y.astype(p_rw.dtype), P['rw_gn_g'][l].reshape(RW_HEADS, RW_HEAD),
                    P['rw_gn_b'][l].reshape(RW_HEADS, RW_HEAD), GN_EPS)
    bonus = jnp.sum(rh * kh * P['rw_rk'][l], axis=-1, keepdims=True) * vh
    out = ((y + bonus).reshape(B, T, D_RW) * g) @ P['w_rw_out'][l]
    return out, v_first, s_new.astype(p_rw.dtype)


def _lin_combine(c1, c2):
    a1, u1 = c1
    a2, u2 = c2
    return a1 * a2, a2 * u1 + u2


def _rglru_branch(l, p_lx, p_lg, conv_buf, h0, P):
    B, T, _ = p_lx.shape
    xpad = jnp.concatenate([conv_buf.astype(p_lx.dtype), p_lx], axis=1)
    w = P['lru_conv_w'][l]
    xc = P['lru_conv_b'][l] + w[CONV_W - 1] * p_lx
    for j in range(CONV_W - 1):
        xc = xc + w[j] * xpad[:, j:j + T]
    xh = xc.reshape(B, T, LRU_BLOCKS, LRU_BW)
    rg = jax.nn.sigmoid(jnp.einsum('btni,nij->btnj', xh, P['lru_w_rg'][l])
                        + P['lru_b_rg'][l].reshape(LRU_BLOCKS, LRU_BW))
    ig = jax.nn.sigmoid(jnp.einsum('btni,nij->btnj', xh, P['lru_w_ig'][l])
                        + P['lru_b_ig'][l].reshape(LRU_BLOCKS, LRU_BW))
    log_a = -LRU_C * rg.astype(f32) * jax.nn.softplus(
        -P['lru_lambda'][l].astype(f32)).reshape(LRU_BLOCKS, LRU_BW)
    a = jnp.exp(log_a).reshape(B, T, D_LRU)
    u = (jnp.sqrt(-jnp.expm1(2.0 * log_a)) * (ig * xh).astype(f32)).reshape(B, T, D_LRU)
    a_cum, u_cum = lax.associative_scan(_lin_combine, (a, u), axis=1)
    h = a_cum * h0.astype(f32)[:, None] + u_cum
    out = (h.astype(p_lg.dtype) * jax.nn.gelu(p_lg)) @ P['w_lru_out'][l]
    return out, xpad[:, T:], h[:, -1].astype(p_lx.dtype)


def _memory_branch(l, q, mem_k, mem_v, P):
    B, T, _ = q.shape
    qh = q.reshape(B, T, XA_HEADS, XA_HEAD)
    s = jnp.einsum('bthd,bmhd->bhtm', qh, mem_k.astype(q.dtype)).astype(f32) * (XA_HEAD ** -0.5)
    p = jax.nn.softmax(s, axis=-1).astype(q.dtype)
    o = jnp.einsum('bhtm,bmhd->bthd', p, mem_v.astype(q.dtype)).reshape(B, T, D_XA)
    return o @ P['w_xa_out'][l]


def _layer(l, x, mem_k, mem_v, shift_prev, s0, conv_buf, h0, v_first, P):
    proj = x @ P['w_in'][l]
    p_rw, p_lx, p_lg, q, gates = jnp.split(proj, IN_SPLITS, axis=-1)
    o_rw, v_first, s_new = _rwkv7_branch(l, p_rw, shift_prev, s0, v_first, P)
    o_lru, conv_new, h_new = _rglru_branch(l, p_lx, p_lg, conv_buf, h0, P)
    o_xa = _memory_branch(l, q, mem_k, mem_v, P)
    g_rw, g_lru, g_xa = jnp.split(jax.nn.sigmoid(gates), N_BRANCH, axis=-1)
    mix = (g_rw * o_rw + g_lru * o_lru + g_xa * o_xa) @ P['w_o'][l]
    x = _layer_norm(ALPHA * x + mix, P['ln1_g'][l], P['ln1_b'][l])
    u, gt = jnp.split(x @ P['w_ffn_in'][l], 2, axis=-1)
    ffn = (jax.nn.silu(gt) * u) @ P['w_ffn_out'][l]
    x = _layer_norm(ALPHA * x + ffn, P['ln2_g'][l], P['ln2_b'][l])
    return x, v_first, p_rw[:, -1], s_new, conv_new, h_new


def _trunk(x, mem_k, mem_v, shift, wkv, conv, h, P):
    v_first = None
    n_sh, n_wkv, n_conv, n_h = [], [], [], []
    for l in range(DEPTH):
        x, v_first, s_sh, s_wkv, s_conv, s_h = _layer(
            l, x, mem_k[l], mem_v[l], shift[l], wkv[l], conv[l], h[l], v_first, P)
        n_sh.append(s_sh)
        n_wkv.append(s_wkv)
        n_conv.append(s_conv)
        n_h.append(s_h)
    return x, jnp.stack(n_sh), jnp.stack(n_wkv), jnp.stack(n_conv), jnp.stack(n_h)


def _nrm(k, shape, scale):
    return jax.random.normal(k, shape, f32) * scale


def setup_inputs(seed: int = 0) -> dict:
    key = jax.random.key(seed)
    ks = iter(jax.random.split(key, 48))
    nk = lambda: next(ks)
    u_lam = jax.random.uniform(nk(), (DEPTH, D_LRU), f32, 0.9, 0.999)
    s_lam = u_lam ** (1.0 / LRU_C)
    d = {}
    d['x_prompt'] = _nrm(nk(), (BATCH, SEQ, D_MODEL), 1.0)
    d['x_sample'] = _nrm(nk(), (DEC_BATCH, DEC_SEQ, D_MODEL), 1.0)
    d['mem_prompt'] = _nrm(nk(), (BATCH, N_MEM, D_MODEL), 1.0)
    d['state_rwkv_shift'] = _nrm(nk(), (DEPTH, DEC_BATCH, RW_COLS), 1.0)
    d['state_rwkv_wkv'] = _nrm(nk(), (DEPTH, DEC_BATCH, RW_HEADS, RW_HEAD, RW_HEAD), 0.3)
    d['state_lru_conv'] = _nrm(nk(), (DEPTH, DEC_BATCH, CONV_W - 1, D_LRU), 1.0)
    d['state_lru_h'] = _nrm(nk(), (DEPTH, DEC_BATCH, D_LRU), 0.5)
    d['cache_mem_k'] = _nrm(nk(), (DEPTH, DEC_BATCH, N_MEM, XA_HEADS, XA_HEAD), 1.0)
    d['cache_mem_v'] = _nrm(nk(), (DEPTH, DEC_BATCH, N_MEM, XA_HEADS, XA_HEAD), 1.0)
    d['w_in'] = _nrm(nk(), (DEPTH, D_MODEL, D_IN), D_MODEL ** -0.5)
    d['rw_mu'] = jax.random.uniform(nk(), (DEPTH, RW_COLS), f32, 0.0, 1.0)
    d['rw_w0'] = jax.random.uniform(nk(), (DEPTH, D_RW), f32, -4.0, 1.0)
    d['rw_w2'] = _nrm(nk(), (DEPTH, W_LORA, D_RW), 0.5 * W_LORA ** -0.5)
    d['rw_a0'] = _nrm(nk(), (DEPTH, D_RW), 0.5)
    d['rw_a2'] = _nrm(nk(), (DEPTH, A_LORA, D_RW), A_LORA ** -0.5)
    d['rw_g2'] = _nrm(nk(), (DEPTH, G_LORA, D_RW), G_LORA ** -0.5)
    d['rw_v0'] = _nrm(nk(), (DEPTH - 1, D_RW), 0.5)
    d['rw_v1'] = _nrm(nk(), (DEPTH - 1, D_RW, V_LORA), D_RW ** -0.5)
    d['rw_v2'] = _nrm(nk(), (DEPTH - 1, V_LORA, D_RW), V_LORA ** -0.5)
    d['rw_kk'] = 0.85 + _nrm(nk(), (DEPTH, D_RW), 0.05)
    d['rw_ka'] = 1.0 + _nrm(nk(), (DEPTH, D_RW), 0.05)
    d['rw_rk'] = _nrm(nk(), (DEPTH, RW_HEADS, RW_HEAD), 0.1)
    d['rw_gn_g'] = 1.0 + _nrm(nk(), (DEPTH, D_RW), 0.02)
    d['rw_gn_b'] = _nrm(nk(), (DEPTH, D_RW), 0.02)
    d['w_rw_out'] = _nrm(nk(), (DEPTH, D_RW, D_MODEL), D_RW ** -0.5)
    d['lru_conv_w'] = _nrm(nk(), (DEPTH, CONV_W, D_LRU), CONV_W ** -0.5)
    d['lru_conv_b'] = _nrm(nk(), (DEPTH, D_LRU), 0.02)
    d['lru_w_rg'] = _nrm(nk(), (DEPTH, LRU_BLOCKS, LRU_BW, LRU_BW), LRU_BW ** -0.5)
    d['lru_b_rg'] = _nrm(nk(), (DEPTH, D_LRU), 0.1)
    d['lru_w_ig'] = _nrm(nk(), (DEPTH, LRU_BLOCKS, LRU_BW, LRU_BW), LRU_BW ** -0.5)
    d['lru_b_ig'] = _nrm(nk(), (DEPTH, D_LRU), 0.1)
    d['lru_lambda'] = jnp.log(s_lam) - jnp.log1p(-s_lam)
    d['w_lru_out'] = _nrm(nk(), (DEPTH, D_LRU, D_MODEL), D_LRU ** -0.5)
    d['w_mem_kv'] = _nrm(nk(), (DEPTH, D_MODEL, 2 * D_XA), D_MODEL ** -0.5)
    d['w_xa_out'] = _nrm(nk(), (DEPTH, D_XA, D_MODEL), D_XA ** -0.5)
    d['w_o'] = _nrm(nk(), (DEPTH, D_MODEL, D_MODEL), BETA * D_MODEL ** -0.5)
    d['ln1_g'] = 1.0 + _nrm(nk(), (DEPTH, D_MODEL), 0.02)
    d['ln1_b'] = _nrm(nk(), (DEPTH, D_MODEL), 0.02)
    d['w_ffn_in'] = _nrm(nk(), (DEPTH, D_MODEL, 2 * D_FF), D_MODEL ** -0.5)
    d['w_ffn_out'] = _nrm(nk(), (DEPTH, D_FF, D_MODEL), BETA * D_FF ** -0.5)
    d['ln2_g'] = 1.0 + _nrm(nk(), (DEPTH, D_MODEL), 0.02)
    d['ln2_b'] = _nrm(nk(), (DEPTH, D_MODEL), 0.02)
    return d


def reference(x_prompt, x_sample, mem_prompt, state_rwkv_shift, state_rwkv_wkv, state_lru_conv,
              state_lru_h, cache_mem_k, cache_mem_v, w_in, rw_mu, rw_w0, rw_w2, rw_a0, rw_a2,
              rw_g2, rw_v0, rw_v1, rw_v2, rw_kk, rw_ka, rw_rk, rw_gn_g, rw_gn_b, w_rw_out,
              lru_conv_w, lru_conv_b, lru_w_rg, lru_b_rg, lru_w_ig, lru_b_ig, lru_lambda, w_lru_out,
              w_mem_kv, w_xa_out, w_o, ln1_g, ln1_b, w_ffn_in, w_ffn_out, ln2_g, ln2_b):
    P = {'w_in': w_in, 'rw_mu': rw_mu, 'rw_w0': rw_w0, 'rw_w2': rw_w2, 'rw_a0': rw_a0,
         'rw_a2': rw_a2, 'rw_g2': rw_g2, 'rw_v0': rw_v0, 'rw_v1': rw_v1, 'rw_v2': rw_v2,
         'rw_kk': rw_kk, 'rw_ka': rw_ka, 'rw_rk': rw_rk, 'rw_gn_g': rw_gn_g, 'rw_gn_b': rw_gn_b,
         'w_rw_out': w_rw_out, 'lru_conv_w': lru_conv_w, 'lru_conv_b': lru_conv_b,
         'lru_w_rg': lru_w_rg, 'lru_b_rg': lru_b_rg, 'lru_w_ig': lru_w_ig, 'lru_b_ig': lru_b_ig,
         'lru_lambda': lru_lambda, 'w_lru_out': w_lru_out, 'w_xa_out': w_xa_out, 'w_o': w_o,
         'ln1_g': ln1_g, 'ln1_b': ln1_b, 'w_ffn_in': w_ffn_in, 'w_ffn_out': w_ffn_out,
         'ln2_g': ln2_g, 'ln2_b': ln2_b}
    Bp = x_prompt.shape[0]
    mk, mv = [], []
    for l in range(DEPTH):
        k_, v_ = jnp.split(mem_prompt @ w_mem_kv[l], 2, axis=-1)
        mk.append(k_.reshape(Bp, N_MEM, XA_HEADS, XA_HEAD))
        mv.append(v_.reshape(Bp, N_MEM, XA_HEADS, XA_HEAD))
    new_mem_k_prompt = jnp.stack(mk)
    new_mem_v_prompt = jnp.stack(mv)
    dt = x_prompt.dtype
    z_shift = jnp.zeros((DEPTH, Bp, RW_COLS), dt)
    z_wkv = jnp.zeros((DEPTH, Bp, RW_HEADS, RW_HEAD, RW_HEAD), dt)
    z_conv = jnp.zeros((DEPTH, Bp, CONV_W - 1, D_LRU), dt)
    z_h = jnp.zeros((DEPTH, Bp, D_LRU), dt)
    y_prompt, sh_p, wkv_p, conv_p, h_p = _trunk(
        x_prompt, new_mem_k_prompt, new_mem_v_prompt, z_shift, z_wkv, z_conv, z_h, P)
    y_sample, sh_s, wkv_s, conv_s, h_s = _trunk(
        x_sample, cache_mem_k, cache_mem_v, state_rwkv_shift, state_rwkv_wkv,
        state_lru_conv, state_lru_h, P)
    return (y_prompt, y_sample, sh_p, wkv_p, conv_p, h_p, new_mem_k_prompt, new_mem_v_prompt,
            sh_s, wkv_s, conv_s, h_s)
```

```python
import functools

import jax
import jax.numpy as jnp
from jax import lax
from jax.experimental import pallas as pl
from jax.experimental.pallas import tpu as pltpu

F32 = jnp.float32
BF16 = jnp.bfloat16

D_MODEL = 1024
DEPTH = 4
N_MEM = 256
RW_HEADS = 12
RW_HEAD = 64
D_RW = RW_HEADS * RW_HEAD
W_LORA = 64
A_LORA = 64
V_LORA = 32
G_LORA = 128
RW_COLS = 3 * D_RW + W_LORA + A_LORA + G_LORA
GN_EPS = 64e-5
LRU_BLOCKS = 12
LRU_BW = 64
D_LRU = LRU_BLOCKS * LRU_BW
CONV_W = 4
LRU_C = 8.0
XA_HEADS = 4
XA_HEAD = 128
D_XA = XA_HEADS * XA_HEAD
N_BRANCH = 3
D_FF = 2816
ALPHA = (2 * DEPTH) ** 0.25
LN_EPS = 1e-5

LANES = 128
SUBLANES = 8
VMEM_LIMIT_BYTES = 52 * 1024 * 1024

ROW_TILE = 256
FFN_CHUNK = 1408
SCAN_T_BLOCK = 32
SCAN_K_UNROLL = 4


def _cparams(*sem):
    return pltpu.CompilerParams(dimension_semantics=sem, vmem_limit_bytes=VMEM_LIMIT_BYTES)


def _const_spec(shape):
    nd = len(shape)
    return pl.BlockSpec(shape, lambda *_: (0,) * nd, pipeline_mode=pl.Buffered(1))


def _dot(a, b):
    return jnp.dot(a, b, preferred_element_type=F32)


def _head_sum(x, ones_ref):
    hi = x.astype(BF16)
    lo = (x - hi.astype(F32)).astype(BF16)
    ones = ones_ref[...]
    return _dot(hi, ones) + _dot(lo, ones)


def _layer_norm_rows(x, g, b, eps):
    mu = jnp.mean(x, axis=-1, keepdims=True)
    d = x - mu
    var = jnp.mean(d * d, axis=-1, keepdims=True)
    return d * lax.rsqrt(var + eps) * g + b


def _delayed_rows(p, hist, j, G):
    n = p.shape[0]
    h = hist.shape[0] // G
    if G % SUBLANES == 0:
        ext = jnp.concatenate([hist, p], axis=0)
        return ext[(h - j) * G:(h - j) * G + n]
    assert G == 1
    out = pltpu.roll(p, j, 0)
    row = lax.broadcasted_iota(jnp.int32, (n, 1), 0)
    for i in range(j):
        out = jnp.where(row == i, hist[h - j + i:h - j + i + 1, :], out)
    return out


def _tail_rows(G, steps):
    return SUBLANES if G == 1 else steps * G


def _mm_kernel(x_ref, w_ref, o_ref):
    o_ref[...] = _dot(x_ref[...], w_ref[...]).astype(o_ref.dtype)


def _matmul(x_bf, w_bf, out_dtype, name):
    n, k = x_bf.shape
    m = w_bf.shape[1]
    tm = min(n, 512)
    return pl.pallas_call(
        _mm_kernel,
        out_shape=jax.ShapeDtypeStruct((n, m), out_dtype),
        grid=(n // tm,),
        in_specs=[pl.BlockSpec((tm, k), lambda i: (i, 0)), _const_spec((k, m))],
        out_specs=pl.BlockSpec((tm, m), lambda i: (i, 0)),
        compiler_params=_cparams("parallel"),
        name=name,
    )(x_bf, w_bf)


def _rw_prep_kernel(first_layer, G, *refs):
    if first_layer:
        (x_ref, wrw_ref, prev_ref, mu_ref, wlora_ref, w0_ref, a0_ref, g2_ref,
         kkp_ref, kap_ref, rk_ref, ones_ref,
         r_out, w_out, k_out, v_out, nkk_out, b_out, g_out, bonus_out, tail_out,
         carry_ref) = refs
    else:
        (x_ref, wrw_ref, prev_ref, mu_ref, wlora_ref, w0_ref, a0_ref, g2_ref,
         kkp_ref, kap_ref, rk_ref, ones_ref, vfirst_ref, v0_ref, v1_ref, v2_ref,
         r_out, w_out, k_out, v_out, nkk_out, b_out, g_out, bonus_out, tail_out,
         carry_ref) = refs

    @pl.when(pl.program_id(1) == 0)
    def _():
        carry_ref[...] = prev_ref[0]

    p = _dot(x_ref[...], wrw_ref[...])
    n = p.shape[0]
    p_prev = _delayed_rows(p, carry_ref[...], 1, G)
    tail = _tail_rows(G, 1)
    carry_ref[...] = p[n - tail:, :]
    tail_out[0] = p[n - tail:, :]
    xs = p + (p_prev - p) * mu_ref[...]

    r = xs[:, 0:D_RW]
    k = xs[:, D_RW:2 * D_RW]
    v = xs[:, 2 * D_RW:3 * D_RW]
    z = xs[:, 3 * D_RW:3 * D_RW + W_LORA + A_LORA]
    gd = xs[:, 3 * D_RW + W_LORA + A_LORA:]

    lane = lax.broadcasted_iota(jnp.int32, (1, W_LORA + A_LORA), 1)
    zt = jnp.where(lane < W_LORA, jnp.tanh(z), z)
    lo = _dot(zt.astype(BF16), wlora_ref[...])
    w = -jax.nn.softplus(-(w0_ref[...] + lo[:, :D_RW])) - 0.5
    decay = jnp.exp(-jnp.exp(w))
    a = jax.nn.sigmoid(a0_ref[...] + lo[:, D_RW:])
    g = _dot(jax.nn.sigmoid(gd).astype(BF16), g2_ref[...])

    if not first_layer:
        vv = _dot(_dot(v.astype(BF16), v1_ref[...]).astype(BF16), v2_ref[...])
        v = v + (vfirst_ref[...] - v) * jax.nn.sigmoid(v0_ref[...] + vv)

    kk = k * kkp_ref[...]
    kk = kk / jnp.maximum(jnp.sqrt(_head_sum(kk * kk, ones_ref)), 1e-12)
    k2 = k * (1.0 + (a - 1.0) * kap_ref[...])
    bonus = _head_sum(r * k2 * rk_ref[...], ones_ref) * v

    r_out[...] = r
    w_out[...] = decay
    k_out[...] = k2
    v_out[...] = v
    nkk_out[...] = -kk
    b_out[...] = kk * a
    g_out[...] = g
    bonus_out[...] = bonus


def _rw_prep(x_bf, prev, wp, l, geom, v_first):
    groups, G, tiles, tm = geom
    n = x_bf.shape[0]
    first = v_first is None
    row = lambda c: pl.BlockSpec((tm, c), lambda b, j: (b * tiles + j, 0))
    vec = lambda c: _const_spec((1, c))
    tail = _tail_rows(G, 1)
    in_specs = [row(D_MODEL), _const_spec((D_MODEL, RW_COLS)),
                pl.BlockSpec((1, tail, RW_COLS), lambda b, j: (b, 0, 0)),
                vec(RW_COLS), _const_spec((W_LORA + A_LORA, 2 * D_RW)), vec(D_RW), vec(D_RW),
                _const_spec((G_LORA, D_RW)), vec(D_RW), vec(D_RW), vec(D_RW),
                _const_spec((D_RW, D_RW))]
    args = [x_bf, wp['w_rw'][l], prev, wp['rw_mu'][l], wp['w_lora'][l], wp['rw_w0'][l],
            wp['rw_a0'][l], wp['rw_g2'][l], wp['rw_kk'][l], wp['rw_ka'][l], wp['rw_rk'][l],
            wp['head_ones']]
    if not first:
        in_specs += [row(D_RW), vec(D_RW), _const_spec((D_RW, LANES)), _const_spec((LANES, D_RW))]
        args += [v_first, wp['rw_v0'][l - 1], wp['rw_v1'][l - 1], wp['rw_v2'][l - 1]]
    out_shape = [jax.ShapeDtypeStruct((n, D_RW), F32)] * 8 + [
        jax.ShapeDtypeStruct((groups, tail, RW_COLS), F32)]
    out_specs = [row(D_RW)] * 8 + [pl.BlockSpec((1, tail, RW_COLS), lambda b, j: (b, 0, 0))]
    return pl.pallas_call(
        functools.partial(_rw_prep_kernel, first, G),
        out_shape=out_shape,
        grid=(groups, tiles),
        in_specs=in_specs,
        out_specs=out_specs,
        scratch_shapes=[pltpu.VMEM((tail, RW_COLS), F32)],
        compiler_params=_cparams("parallel", "arbitrary"),
        name="rw_prep",
    )(*args)


def _wkv_scan_kernel(r_ref, w_ref, k_ref, v_ref, nkk_ref, b_ref, s0_ref, y_ref, s_out_ref):
    tt = r_ref.shape[0]
    vt = RW_HEAD // SUBLANES

    @pl.when(pl.program_id(1) == 0)
    def _():
        s_out_ref[...] = s0_ref[...]

    def bcast(ref, t, kidx):
        return jnp.broadcast_to(ref[t, pl.ds(kidx, 1), :], (SUBLANES, LANES))

    def step(t, carry):
        def sa_body(kidx, sa):
            nk = bcast(nkk_ref, t, kidx)
            return tuple(sa[i] + s_out_ref[kidx, i * SUBLANES:(i + 1) * SUBLANES, :] * nk
                         for i in range(vt))
        zeros = tuple(jnp.zeros((SUBLANES, LANES), F32) for _ in range(vt))
        sa = lax.fori_loop(0, RW_HEAD, sa_body, zeros, unroll=SCAN_K_UNROLL)
        vv = tuple(v_ref[t, i * SUBLANES:(i + 1) * SUBLANES, :] for i in range(vt))

        def up_body(kidx, y):
            wk = bcast(w_ref, t, kidx)
            bk = bcast(b_ref, t, kidx)
            kk = bcast(k_ref, t, kidx)
            rk = bcast(r_ref, t, kidx)
            out = []
            for i in range(vt):
                sl = slice(i * SUBLANES, (i + 1) * SUBLANES)
                s_new = s_out_ref[kidx, sl, :] * wk + sa[i] * bk + vv[i] * kk
                s_out_ref[kidx, sl, :] = s_new
                out.append(y[i] + s_new * rk)
            return tuple(out)
        y = lax.fori_loop(0, RW_HEAD, up_body, zeros, unroll=SCAN_K_UNROLL)
        for i in range(vt):
            y_ref[t, i * SUBLANES:(i + 1) * SUBLANES, :] = y[i]
        return carry

    lax.fori_loop(0, tt, step, 0)


def _wkv_scan(ops, s0):
    t_len, _, lanes = ops[0].shape
    tt = min(t_len, SCAN_T_BLOCK)
    op_spec = pl.BlockSpec((tt, RW_HEAD, LANES), lambda g, j: (j, 0, g))
    st_spec = pl.BlockSpec((RW_HEAD, RW_HEAD, LANES), lambda g, j: (0, 0, g))
    return pl.pallas_call(
        _wkv_scan_kernel,
        out_shape=[jax.ShapeDtypeStruct((t_len, RW_HEAD, lanes), F32),
                   jax.ShapeDtypeStruct((RW_HEAD, RW_HEAD, lanes), F32)],
        grid=(lanes // LANES, t_len // tt),
        in_specs=[op_spec] * 6 + [st_spec],
        out_specs=[op_spec, st_spec],
        compiler_params=_cparams("parallel", "arbitrary"),
        name="wkv_scan",
    )(*ops, s0)


def _lru_kernel(G, x_ref, wl_ref, conv0_ref, h0_ref, cw_ref, cb_ref, wgate_ref, brg_ref, big_ref,
                lam_ref, hg_out, conv_out, h_out, hist_ref, hcar_ref, a_scr, u_scr, h_scr):
    hist_steps = CONV_W - 1

    @pl.when(pl.program_id(1) == 0)
    def _():
        hist_ref[...] = conv0_ref[0]
        hcar_ref[...] = h0_ref[0]

    p = _dot(x_ref[...], wl_ref[...])
    n = p.shape[0]
    p_lx = p[:, :D_LRU]
    p_lg = p[:, D_LRU:]
    hist = hist_ref[...]
    cw = cw_ref[...]
    xc = cb_ref[...] + cw[CONV_W - 1:CONV_W, :] * p_lx
    for j in range(hist_steps):
        xc = xc + cw[j:j + 1, :] * _delayed_rows(p_lx, hist, hist_steps - j, G)
    tail = _tail_rows(G, hist_steps)
    hist_ref[...] = p_lx[n - tail:, :]
    conv_out[0] = p_lx[n - tail:, :]

    gates = _dot(xc.astype(BF16), wgate_ref[...])
    rg = jax.nn.sigmoid(gates[:, :D_LRU] + brg_ref[...])
    ig = jax.nn.sigmoid(gates[:, D_LRU:] + big_ref[...])
    log_a = -LRU_C * rg * jax.nn.softplus(-lam_ref[...])
    a = jnp.exp(log_a)
    a_scr[...] = a
    u_scr[...] = jnp.sqrt(-jnp.tanh(log_a) * (a * a + 1.0)) * (ig * xc)

    steps = n // G

    def step(s, h):
        rows = pl.ds(pl.multiple_of(s * G, G), G)
        h = a_scr[rows, :] * h + u_scr[rows, :]
        h_scr[rows, :] = h
        return h
    h = lax.fori_loop(0, steps, step, hcar_ref[...], unroll=min(steps, 8))
    hcar_ref[...] = h
    h_out[0] = h
    hg_out[...] = (h_scr[...] * jax.nn.gelu(p_lg)).astype(hg_out.dtype)


def _lru(x_bf, conv0, h0, wp, l, geom):
    groups, G, tiles, tm = geom
    n = x_bf.shape[0]
    hist_steps = CONV_W - 1
    tail = _tail_rows(G, hist_steps)
    row = lambda c: pl.BlockSpec((tm, c), lambda b, j: (b * tiles + j, 0))
    vec = lambda c: _const_spec((1, c))
    grp = lambda r, c: pl.BlockSpec((1, r, c), lambda b, j: (b, 0, 0))
    return pl.pallas_call(
        functools.partial(_lru_kernel, G),
        out_shape=[jax.ShapeDtypeStruct((n, D_LRU), BF16),
                   jax.ShapeDtypeStruct((groups, tail, D_LRU), F32),
                   jax.ShapeDtypeStruct((groups, G, D_LRU), F32)],
        grid=(groups, tiles),
        in_specs=[row(D_MODEL), _const_spec((D_MODEL, 2 * D_LRU)),
                  grp(tail, D_LRU), grp(G, D_LRU),
                  _const_spec((CONV_W, D_LRU)), vec(D_LRU), _const_spec((D_LRU, 2 * D_LRU)),
                  vec(D_LRU), vec(D_LRU), vec(D_LRU)],
        out_specs=[row(D_LRU), grp(tail, D_LRU), grp(G, D_LRU)],
        scratch_shapes=[pltpu.VMEM((tail, D_LRU), F32), pltpu.VMEM((G, D_LRU), F32),
                        pltpu.VMEM((tm, D_LRU), F32), pltpu.VMEM((tm, D_LRU), F32),
                        pltpu.VMEM((tm, D_LRU), F32)],
        compiler_params=_cparams("parallel", "arbitrary"),
        name="rg_lru",
    )(x_bf, wp['w_lru'][l], conv0, h0, wp['lru_conv_w'][l], wp['lru_conv_b'][l], wp['w_lru_gate'][l],
      wp['lru_b_rg'][l], wp['lru_b_ig'][l], wp['lru_lambda'][l])


def _attn_kernel(q_ref, k_ref, v_ref, o_ref):
    scale = XA_HEAD ** -0.5
    for h in range(XA_HEADS):
        sl = slice(h * XA_HEAD, (h + 1) * XA_HEAD)
        q = q_ref[:, :, sl]
        k = k_ref[:, :, sl].astype(BF16)
        v = v_ref[:, :, sl].astype(BF16)
        s = jnp.einsum('bqd,bkd->bqk', q, k, preferred_element_type=F32) * scale
        s = s - jnp.max(s, axis=-1, keepdims=True)
        e = jnp.exp(s)
        p = (e / jnp.sum(e, axis=-1, keepdims=True)).astype(BF16)
        o = jnp.einsum('bqk,bkd->bqd', p, v, preferred_element_type=F32)
        o_ref[:, :, sl] = o.astype(o_ref.dtype)


def _attention(q3, k3, v3, bb, tq):
    b, t, _ = q3.shape
    qspec = pl.BlockSpec((bb, tq, D_XA), lambda i, j: (i, j, 0))
    kspec = pl.BlockSpec((bb, N_MEM, D_XA), lambda i, j: (i, 0, 0))
    return pl.pallas_call(
        _attn_kernel,
        out_shape=jax.ShapeDtypeStruct((b, t, D_XA), BF16),
        grid=(b // bb, t // tq),
        in_specs=[qspec, kspec, kspec],
        out_specs=qspec,
        compiler_params=_cparams("parallel", "parallel"),
        name="mem_attn",
    )(q3, k3, v3)


def _merge_kernel(x_ref, xbf_ref, wg_ref, y_ref, bonus_ref, g_ref, gng_ref, gnb_ref, hg_ref,
                  oatt_ref, wrw_ref, wlru_ref, wxa_ref, wo_ref, ones_ref, lng_ref, lnb_ref,
                  xo_ref, xobf_ref):
    gates = jax.nn.sigmoid(_dot(xbf_ref[...], wg_ref[...]))
    y = y_ref[...]
    inv = 1.0 / RW_HEAD
    d = y - _head_sum(y, ones_ref) * inv
    var = _head_sum(d * d, ones_ref) * inv
    yn = d * lax.rsqrt(var + GN_EPS) * gng_ref[...] + gnb_ref[...]
    o_rw = _dot(((yn + bonus_ref[...]) * g_ref[...]).astype(BF16), wrw_ref[...])
    o_lru = _dot(hg_ref[...], wlru_ref[...])
    o_xa = _dot(oatt_ref[...], wxa_ref[...])
    m = (gates[:, :D_MODEL] * o_rw + gates[:, D_MODEL:2 * D_MODEL] * o_lru
         + gates[:, 2 * D_MODEL:] * o_xa)
    mix = _dot(m.astype(BF16), wo_ref[...])
    xo = _layer_norm_rows(ALPHA * x_ref[...] + mix, lng_ref[...], lnb_ref[...], LN_EPS)
    xo_ref[...] = xo
    xobf_ref[...] = xo.astype(BF16)


def _merge(x, x_bf, y, bonus, g, hg, oatt, wp, l, tm):
    n = x.shape[0]
    row = lambda c: pl.BlockSpec((tm, c), lambda i: (i, 0))
    vec = lambda c: _const_spec((1, c))
    return pl.pallas_call(
        _merge_kernel,
        out_shape=[jax.ShapeDtypeStruct((n, D_MODEL), F32), jax.ShapeDtypeStruct((n, D_MODEL), BF16)],
        grid=(n // tm,),
        in_specs=[row(D_MODEL), row(D_MODEL), _const_spec((D_MODEL, N_BRANCH * D_MODEL)),
                  row(D_RW), row(D_RW), row(D_RW), vec(D_RW), vec(D_RW), row(D_LRU), row(D_XA),
                  _const_spec((D_RW, D_MODEL)), _const_spec((D_LRU, D_MODEL)),
                  _const_spec((D_XA, D_MODEL)), _const_spec((D_MODEL, D_MODEL)),
                  _const_spec((D_RW, D_RW)), vec(D_MODEL), vec(D_MODEL)],
        out_specs=[row(D_MODEL), row(D_MODEL)],
        compiler_params=_cparams("parallel"),
        name="merge_ln1",
    )(x, x_bf, wp['w_gate'][l], y, bonus, g, wp['rw_gn_g'][l], wp['rw_gn_b'][l], hg, oatt,
      wp['w_rw_out'][l], wp['w_lru_out'][l], wp['w_xa_out'][l], wp['w_o'][l], wp['head_ones'],
      wp['ln1_g'][l], wp['ln1_b'][l])


def _ffn_kernel(x_ref, xbf_ref, win_ref, wout_ref, lng_ref, lnb_ref, xo_ref, xobf_ref):
    xb = xbf_ref[...]
    acc = None
    for c in range(D_FF // FFN_CHUNK):
        u = _dot(xb, win_ref[:, c * FFN_CHUNK:(c + 1) * FFN_CHUNK])
        gt = _dot(xb, win_ref[:, D_FF + c * FFN_CHUNK:D_FF + (c + 1) * FFN_CHUNK])
        part = _dot((jax.nn.silu(gt) * u).astype(BF16), wout_ref[c * FFN_CHUNK:(c + 1) * FFN_CHUNK, :])
        acc = part if acc is None else acc + part
    xo = _layer_norm_rows(ALPHA * x_ref[...] + acc, lng_ref[...], lnb_ref[...], LN_EPS)
    xo_ref[...] = xo
    xobf_ref[...] = xo.astype(BF16)


def _ffn(x, x_bf, wp, l, tm):
    n = x.shape[0]
    row = lambda c: pl.BlockSpec((tm, c), lambda i: (i, 0))
    vec = lambda c: _const_spec((1, c))
    return pl.pallas_call(
        _ffn_kernel,
        out_shape=[jax.ShapeDtypeStruct((n, D_MODEL), F32), jax.ShapeDtypeStruct((n, D_MODEL), BF16)],
        grid=(n // tm,),
        in_specs=[row(D_MODEL), row(D_MODEL), _const_spec((D_MODEL, 2 * D_FF)),
                  _const_spec((D_FF, D_MODEL)), vec(D_MODEL), vec(D_MODEL)],
        out_specs=[row(D_MODEL), row(D_MODEL)],
        compiler_params=_cparams("parallel"),
        name="ffn_ln2",
    )(x, x_bf, wp['w_ffn_in'][l], wp['w_ffn_out'][l], wp['ln2_g'][l], wp['ln2_b'][l])


def _block_diag(blocks):
    n, bi, bj = blocks.shape
    eye = jnp.eye(n, dtype=blocks.dtype)
    return (eye[:, None, :, None] * blocks[:, :, None, :]).reshape(n * bi, n * bj)


def _prepare_weights(P):
    rowvec = lambda a: a.reshape(a.shape[0], 1, -1).astype(F32)
    w_in = P['w_in']
    c0, c1, c2, c3 = RW_COLS, RW_COLS + D_LRU, RW_COLS + 2 * D_LRU, RW_COLS + 2 * D_LRU + D_XA
    zeros_wa = jnp.zeros((DEPTH, W_LORA, D_RW), F32)
    wp = {
        'w_rw': w_in[:, :, :c0].astype(BF16),
        'w_lru': w_in[:, :, c0:c2].astype(BF16),
        'w_q': w_in[:, :, c2:c3].astype(BF16),
        'w_gate': w_in[:, :, c3:].astype(BF16),
        'w_lora': jnp.concatenate(
            [jnp.concatenate([P['rw_w2'], zeros_wa], axis=2),
             jnp.concatenate([zeros_wa, P['rw_a2']], axis=2)], axis=1).astype(BF16),
        'rw_g2': P['rw_g2'].astype(BF16),
        'rw_v1': jnp.pad(P['rw_v1'], ((0, 0), (0, 0), (0, LANES - V_LORA))).astype(BF16),
        'rw_v2': jnp.pad(P['rw_v2'], ((0, 0), (0, LANES - V_LORA), (0, 0))).astype(BF16),
        'rw_rk': rowvec(P['rw_rk'].reshape(DEPTH, D_RW)),
        'head_ones': _block_diag(jnp.ones((RW_HEADS, RW_HEAD, RW_HEAD), F32)).astype(BF16),
        'w_lru_gate': jnp.concatenate(
            [jax.vmap(_block_diag)(P['lru_w_rg']), jax.vmap(_block_diag)(P['lru_w_ig'])],
            axis=2).astype(BF16),
        'lru_conv_w': P['lru_conv_w'].astype(F32),
        'w_mem_kv': P['w_mem_kv'].astype(BF16),
    }
    for name in ('w_rw_out', 'w_lru_out', 'w_xa_out', 'w_o', 'w_ffn_in', 'w_ffn_out'):
        wp[name] = P[name].astype(BF16)
    for name in ('rw_mu', 'rw_w0', 'rw_a0', 'rw_v0', 'rw_kk', 'rw_ka', 'rw_gn_g', 'rw_gn_b',
                 'lru_conv_b', 'lru_b_rg', 'lru_b_ig', 'lru_lambda', 'ln1_g', 'ln1_b', 'ln2_g', 'ln2_b'):
        wp[name] = rowvec(P[name])
    return wp


def _to_scan_layout(a, batch, t_len, time_major):
    if time_major:
        a = a.reshape(t_len, batch, RW_HEADS, RW_HEAD).transpose(0, 3, 1, 2)
    else:
        a = a.reshape(batch, t_len, RW_HEADS, RW_HEAD).transpose(1, 3, 0, 2)
    a = a.reshape(t_len, RW_HEAD, batch * RW_HEADS)
    pad = -(batch * RW_HEADS) % LANES
    return jnp.pad(a, ((0, 0), (0, 0), (0, pad)))


def _from_scan_layout(y, batch, t_len, time_major):
    y = y[:, :, :batch * RW_HEADS].reshape(t_len, RW_HEAD, batch, RW_HEADS)
    y = y.transpose(0, 2, 3, 1) if time_major else y.transpose(2, 0, 3, 1)
    return y.reshape(batch * t_len, D_RW)


def _state_to_scan_layout(s, batch):
    s = s.transpose(3, 2, 0, 1).reshape(RW_HEAD, RW_HEAD, batch * RW_HEADS)
    pad = -(batch * RW_HEADS) % LANES
    return jnp.pad(s, ((0, 0), (0, 0), (0, pad)))


def _state_from_scan_layout(s, batch):
    s = s[:, :, :batch * RW_HEADS].reshape(RW_HEAD, RW_HEAD, batch, RW_HEADS)
    return s.transpose(2, 3, 1, 0)


def _trunk(x3, mem_k, mem_v, shift, wkv, conv, h, wp, time_major):
    batch, t_len, _ = x3.shape
    assert t_len >= CONV_W - 1
    n = batch * t_len
    hist_steps = CONV_W - 1
    if time_major:
        groups, G, tiles, tm = 1, batch, 1, n
        x = x3.transpose(1, 0, 2).reshape(n, D_MODEL)
        tm_rows = min(n, ROW_TILE)
    else:
        tm = min(t_len, ROW_TILE)
        groups, G, tiles = batch, 1, t_len // tm
        x = x3.reshape(n, D_MODEL)
        tm_rows = tm
    geom = (groups, G, tiles, tm)
    x_bf = x.astype(BF16)
    v_first = None
    n_sh, n_wkv, n_conv, n_h = [], [], [], []
    for l in range(DEPTH):
        if time_major:
            prev = shift[l][None]
            conv0 = conv[l].transpose(1, 0, 2).reshape(1, hist_steps * batch, D_LRU)
            h0 = h[l][None]
        else:
            prev = jnp.pad(shift[l][:, None, :], ((0, 0), (SUBLANES - 1, 0), (0, 0)))
            conv0 = jnp.pad(conv[l], ((0, 0), (SUBLANES - hist_steps, 0), (0, 0)))
            h0 = h[l][:, None, :]
        r, w, k, v, nkk, b, g, bonus, sh_tail = _rw_prep(x_bf, prev, wp, l, geom, v_first)
        if l == 0:
            v_first = v
        ops = [_to_scan_layout(a, batch, t_len, time_major) for a in (r, w, k, v, nkk, b)]
        y_s, s_new = _wkv_scan(ops, _state_to_scan_layout(wkv[l], batch))
        y = _from_scan_layout(y_s, batch, t_len, time_major)
        n_wkv.append(_state_from_scan_layout(s_new, batch))
        n_sh.append(sh_tail[0] if time_major else sh_tail[:, -1, :])
        hg, conv_tail, h_last = _lru(x_bf, conv0, h0, wp, l, geom)
        if time_major:
            n_conv.append(conv_tail[0].reshape(hist_steps, batch, D_LRU).transpose(1, 0, 2))
            n_h.append(h_last[0])
        else:
            n_conv.append(conv_tail[:, SUBLANES - hist_steps:, :])
            n_h.append(h_last[:, 0, :])
        q = _matmul(x_bf, wp['w_q'][l], BF16, "q_proj")
        if time_major:
            q3 = q.reshape(t_len, batch, D_XA).transpose(1, 0, 2)
            t_pad = -t_len % (2 * SUBLANES)
            q3 = jnp.pad(q3, ((0, 0), (0, t_pad), (0, 0)))
            o3 = _attention(q3, mem_k[l], mem_v[l], SUBLANES, t_len + t_pad)
            oatt = o3[:, :t_len].transpose(1, 0, 2).reshape(n, D_XA)
        else:
            q3 = q.reshape(batch, t_len, D_XA)
            oatt = _attention(q3, mem_k[l], mem_v[l], 1, min(t_len, 1024)).reshape(n, D_XA)
        x, x_bf = _merge(x, x_bf, y, bonus, g, hg, oatt, wp, l, tm_rows)
        x, x_bf = _ffn(x, x_bf, wp, l, tm_rows)
    if time_major:
        y3 = x.reshape(t_len, batch, D_MODEL).transpose(1, 0, 2)
    else:
        y3 = x.reshape(batch, t_len, D_MODEL)
    return y3, jnp.stack(n_sh), jnp.stack(n_wkv), jnp.stack(n_conv), jnp.stack(n_h)


def kernel(x_prompt, x_sample, mem_prompt, state_rwkv_shift, state_rwkv_wkv, state_lru_conv,
           state_lru_h, cache_mem_k, cache_mem_v, w_in, rw_mu, rw_w0, rw_w2, rw_a0, rw_a2,
           rw_g2, rw_v0, rw_v1, rw_v2, rw_kk, rw_ka, rw_rk, rw_gn_g, rw_gn_b, w_rw_out,
           lru_conv_w, lru_conv_b, lru_w_rg, lru_b_rg, lru_w_ig, lru_b_ig, lru_lambda, w_lru_out,
           w_mem_kv, w_xa_out, w_o, ln1_g, ln1_b, w_ffn_in, w_ffn_out, ln2_g, ln2_b):
    P = dict(w_in=w_in, rw_mu=rw_mu, rw_w0=rw_w0, rw_w2=rw_w2, rw_a0=rw_a0, rw_a2=rw_a2,
             rw_g2=rw_g2, rw_v0=rw_v0, rw_v1=rw_v1, rw_v2=rw_v2, rw_kk=rw_kk, rw_ka=rw_ka,
             rw_rk=rw_rk, rw_gn_g=rw_gn_g, rw_gn_b=rw_gn_b, w_rw_out=w_rw_out,
             lru_conv_w=lru_conv_w, lru_conv_b=lru_conv_b, lru_w_rg=lru_w_rg, lru_b_rg=lru_b_rg,
             lru_w_ig=lru_w_ig, lru_b_ig=lru_b_ig, lru_lambda=lru_lambda, w_lru_out=w_lru_out,
             w_mem_kv=w_mem_kv, w_xa_out=w_xa_out, w_o=w_o, ln1_g=ln1_g, ln1_b=ln1_b,
             w_ffn_in=w_ffn_in, w_ffn_out=w_ffn_out, ln2_g=ln2_g, ln2_b=ln2_b)
    wp = _prepare_weights(P)

    bp, n_mem, _ = mem_prompt.shape
    mem_bf = mem_prompt.reshape(bp * n_mem, D_MODEL).astype(BF16)
    mk, mv = [], []
    for l in range(DEPTH):
        kv = _matmul(mem_bf, wp['w_mem_kv'][l], F32, "mem_kv")
        mk.append(kv[:, :D_XA].reshape(bp, n_mem, D_XA))
        mv.append(kv[:, D_XA:].reshape(bp, n_mem, D_XA))
    new_mem_k = jnp.stack(mk).reshape(DEPTH, bp, n_mem, XA_HEADS, XA_HEAD)
    new_mem_v = jnp.stack(mv).reshape(DEPTH, bp, n_mem, XA_HEADS, XA_HEAD)
    dt = x_prompt.dtype
    z_shift = jnp.zeros((DEPTH, bp, RW_COLS), dt)
    z_wkv = jnp.zeros((DEPTH, bp, RW_HEADS, RW_HEAD, RW_HEAD), dt)
    z_conv = jnp.zeros((DEPTH, bp, CONV_W - 1, D_LRU), dt)
    z_h = jnp.zeros((DEPTH, bp, D_LRU), dt)
    y_prompt, sh_p, wkv_p, conv_p, h_p = _trunk(
        x_prompt, mk, mv, z_shift, z_wkv, z_conv, z_h, wp, time_major=False)

    bs = x_sample.shape[0]
    ck = cache_mem_k.reshape(DEPTH, bs, n_mem, D_XA)
    cv = cache_mem_v.reshape(DEPTH, bs, n_mem, D_XA)
    y_sample, sh_s, wkv_s, conv_s, h_s = _trunk(
        x_sample, ck, cv, state_rwkv_shift, state_rwkv_wkv, state_lru_conv, state_lru_h, wp,
        time_major=True)
    return (y_prompt, y_sample, sh_p, wkv_p, conv_p, h_p, new_mem_k, new_mem_v,
            sh_s, wkv_s, conv_s, h_s)
```

```python
import functools

import jax
import jax.numpy as jnp
from jax import lax
from jax.experimental import pallas as pl
from jax.experimental.pallas import tpu as pltpu

F32 = jnp.float32
BF16 = jnp.bfloat16

D_MODEL = 1024
DEPTH = 4
N_MEM = 256
RW_HEADS = 12
RW_HEAD = 64
D_RW = RW_HEADS * RW_HEAD
W_LORA = 64
A_LORA = 64
V_LORA = 32
G_LORA = 128
RW_COLS = 3 * D_RW + W_LORA + A_LORA + G_LORA
GN_EPS = 64e-5
LRU_BLOCKS = 12
LRU_BW = 64
D_LRU = LRU_BLOCKS * LRU_BW
CONV_W = 4
LRU_C = 8.0
XA_HEADS = 4
XA_HEAD = 128
D_XA = XA_HEADS * XA_HEAD
N_BRANCH = 3
D_FF = 2816
ALPHA = (2 * DEPTH) ** 0.25
LN_EPS = 1e-5

LANES = 128
SUBLANES = 8
VMEM_LIMIT_BYTES = 52 * 1024 * 1024

ROW_TILE = 256
FFN_CHUNK = 1408
SCAN_T_BLOCK = 32
SCAN_K_UNROLL = 4
WKV_CHUNK_LOG2 = 6
WKV_CHUNK = 1 << WKV_CHUNK_LOG2
WKV_ROW_TILE = 512


def _cparams(*sem):
    return pltpu.CompilerParams(dimension_semantics=sem, vmem_limit_bytes=VMEM_LIMIT_BYTES)


def _const_spec(shape):
    nd = len(shape)
    return pl.BlockSpec(shape, lambda *_: (0,) * nd, pipeline_mode=pl.Buffered(1))


def _dot(a, b):
    return jnp.dot(a, b, preferred_element_type=F32)


def _split_bf16(x):
    hi = x.astype(BF16)
    return hi, (x - hi.astype(F32)).astype(BF16)


def _head_sum(x, red_ref, exp_ref):
    hi, lo = _split_bf16(x)
    red = red_ref[...]
    s_hi, s_lo = _split_bf16(_dot(hi, red) + _dot(lo, red))
    ex = exp_ref[...]
    return _dot(s_hi, ex) + _dot(s_lo, ex)


def _layer_norm_rows(x, g, b, eps):
    mu = jnp.mean(x, axis=-1, keepdims=True)
    d = x - mu
    var = jnp.mean(d * d, axis=-1, keepdims=True)
    return d * lax.rsqrt(var + eps) * g + b


def _delayed_rows(p, hist, j, G):
    n = p.shape[0]
    h = hist.shape[0] // G
    if G % SUBLANES == 0:
        ext = jnp.concatenate([hist, p], axis=0)
        return ext[(h - j) * G:(h - j) * G + n]
    assert G == 1
    out = pltpu.roll(p, j, 0)
    row = lax.broadcasted_iota(jnp.int32, (n, 1), 0)
    for i in range(j):
        out = jnp.where(row == i, hist[h - j + i:h - j + i + 1, :], out)
    return out


def _tail_rows(G, steps):
    return SUBLANES if G == 1 else steps * G


def _mm_kernel(x_ref, w_ref, o_ref):
    o_ref[...] = _dot(x_ref[...], w_ref[...]).astype(o_ref.dtype)


def _matmul(x_bf, w_bf, out_dtype, name):
    n, k = x_bf.shape
    m = w_bf.shape[1]
    tm = min(n, 512)
    return pl.pallas_call(
        _mm_kernel,
        out_shape=jax.ShapeDtypeStruct((n, m), out_dtype),
        grid=(n // tm,),
        in_specs=[pl.BlockSpec((tm, k), lambda i: (i, 0)), _const_spec((k, m))],
        out_specs=pl.BlockSpec((tm, m), lambda i: (i, 0)),
        compiler_params=_cparams("parallel"),
        name=name,
    )(x_bf, w_bf)


def _rw_prep_kernel(first_layer, G, log_decay, *refs):
    if first_layer:
        (x_ref, wrw_ref, prev_ref, mu_ref, wlora_ref, w0_ref, a0_ref, g2_ref,
         kkp_ref, kap_ref, rk_ref, red_ref, exp_ref,
         r_out, w_out, k_out, v_out, nkk_out, b_out, g_out, bonus_out, tail_out,
         carry_ref) = refs
    else:
        (x_ref, wrw_ref, prev_ref, mu_ref, wlora_ref, w0_ref, a0_ref, g2_ref,
         kkp_ref, kap_ref, rk_ref, red_ref, exp_ref, vfirst_ref, v0_ref, v1_ref, v2_ref,
         r_out, w_out, k_out, v_out, nkk_out, b_out, g_out, bonus_out, tail_out,
         carry_ref) = refs

    @pl.when(pl.program_id(1) == 0)
    def _():
        carry_ref[...] = prev_ref[0]

    p = _dot(x_ref[...], wrw_ref[...])
    n = p.shape[0]
    p_prev = _delayed_rows(p, carry_ref[...], 1, G)
    tail = _tail_rows(G, 1)
    carry_ref[...] = p[n - tail:, :]
    tail_out[0] = p[n - tail:, :]
    xs = p + (p_prev - p) * mu_ref[...]

    r = xs[:, 0:D_RW]
    k = xs[:, D_RW:2 * D_RW]
    v = xs[:, 2 * D_RW:3 * D_RW]
    z = xs[:, 3 * D_RW:3 * D_RW + W_LORA + A_LORA]
    gd = xs[:, 3 * D_RW + W_LORA + A_LORA:]

    lane = lax.broadcasted_iota(jnp.int32, (1, W_LORA + A_LORA), 1)
    zt = jnp.where(lane < W_LORA, jnp.tanh(z), z)
    lo = _dot(zt.astype(BF16), wlora_ref[...])
    w = -jax.nn.softplus(-(w0_ref[...] + lo[:, :D_RW])) - 0.5
    decay = -jnp.exp(w) if log_decay else jnp.exp(-jnp.exp(w))
    a = jax.nn.sigmoid(a0_ref[...] + lo[:, D_RW:])
    g = _dot(jax.nn.sigmoid(gd).astype(BF16), g2_ref[...])

    if not first_layer:
        vv = _dot(_dot(v.astype(BF16), v1_ref[...]).astype(BF16), v2_ref[...])
        v = v + (vfirst_ref[...] - v) * jax.nn.sigmoid(v0_ref[...] + vv)

    kk = k * kkp_ref[...]
    kk = kk / jnp.maximum(jnp.sqrt(_head_sum(kk * kk, red_ref, exp_ref)), 1e-12)
    k2 = k * (1.0 + (a - 1.0) * kap_ref[...])
    bonus = _head_sum(r * k2 * rk_ref[...], red_ref, exp_ref) * v

    r_out[...] = r
    w_out[...] = decay
    k_out[...] = k2
    v_out[...] = v
    nkk_out[...] = -kk
    b_out[...] = kk * a
    g_out[...] = g
    bonus_out[...] = bonus


def _rw_prep(x_bf, prev, wp, l, geom, v_first, log_decay):
    groups, G, tiles, tm = geom
    n = x_bf.shape[0]
    first = v_first is None
    row = lambda c: pl.BlockSpec((tm, c), lambda b, j: (b * tiles + j, 0))
    vec = lambda c: _const_spec((1, c))
    tail = _tail_rows(G, 1)
    in_specs = [row(D_MODEL), _const_spec((D_MODEL, RW_COLS)),
                pl.BlockSpec((1, tail, RW_COLS), lambda b, j: (b, 0, 0)),
                vec(RW_COLS), _const_spec((W_LORA + A_LORA, 2 * D_RW)), vec(D_RW), vec(D_RW),
                _const_spec((G_LORA, D_RW)), vec(D_RW), vec(D_RW), vec(D_RW),
                _const_spec((D_RW, LANES)), _const_spec((LANES, D_RW))]
    args = [x_bf, wp['w_rw'][l], prev, wp['rw_mu'][l], wp['w_lora'][l], wp['rw_w0'][l],
            wp['rw_a0'][l], wp['rw_g2'][l], wp['rw_kk'][l], wp['rw_ka'][l], wp['rw_rk'][l],
            wp['head_red'], wp['head_exp']]
    if not first:
        in_specs += [row(D_RW), vec(D_RW), _const_spec((D_RW, LANES)), _const_spec((LANES, D_RW))]
        args += [v_first, wp['rw_v0'][l - 1], wp['rw_v1'][l - 1], wp['rw_v2'][l - 1]]
    out_shape = [jax.ShapeDtypeStruct((n, D_RW), F32)] * 8 + [
        jax.ShapeDtypeStruct((groups, tail, RW_COLS), F32)]
    out_specs = [row(D_RW)] * 8 + [pl.BlockSpec((1, tail, RW_COLS), lambda b, j: (b, 0, 0))]
    return pl.pallas_call(
        functools.partial(_rw_prep_kernel, first, G, log_decay),
        out_shape=out_shape,
        grid=(groups, tiles),
        in_specs=in_specs,
        out_specs=out_specs,
        scratch_shapes=[pltpu.VMEM((tail, RW_COLS), F32)],
        compiler_params=_cparams("parallel", "arbitrary"),
        name="rw_prep",
    )(*args)


def _wkv_scan_kernel(r_ref, w_ref, k_ref, v_ref, nkk_ref, b_ref, s0_ref, y_ref, s_out_ref):
    tt = r_ref.shape[0]
    vt = RW_HEAD // SUBLANES

    @pl.when(pl.program_id(1) == 0)
    def _():
        s_out_ref[...] = s0_ref[...]

    def bcast(ref, t, kidx):
        return jnp.broadcast_to(ref[t, pl.ds(kidx, 1), :], (SUBLANES, LANES))

    def step(t, carry):
        def sa_body(kidx, sa):
            nk = bcast(nkk_ref, t, kidx)
            return tuple(sa[i] + s_out_ref[kidx, i * SUBLANES:(i + 1) * SUBLANES, :] * nk
                         for i in range(vt))
        zeros = tuple(jnp.zeros((SUBLANES, LANES), F32) for _ in range(vt))
        sa = lax.fori_loop(0, RW_HEAD, sa_body, zeros, unroll=SCAN_K_UNROLL)
        vv = tuple(v_ref[t, i * SUBLANES:(i + 1) * SUBLANES, :] for i in range(vt))

        def up_body(kidx, y):
            wk = bcast(w_ref, t, kidx)
            bk = bcast(b_ref, t, kidx)
            kk = bcast(k_ref, t, kidx)
            rk = bcast(r_ref, t, kidx)
            out = []
            for i in range(vt):
                sl = slice(i * SUBLANES, (i + 1) * SUBLANES)
                s_new = s_out_ref[kidx, sl, :] * wk + sa[i] * bk + vv[i] * kk
                s_out_ref[kidx, sl, :] = s_new
                out.append(y[i] + s_new * rk)
            return tuple(out)
        y = lax.fori_loop(0, RW_HEAD, up_body, zeros, unroll=SCAN_K_UNROLL)
        for i in range(vt):
            y_ref[t, i * SUBLANES:(i + 1) * SUBLANES, :] = y[i]
        return carry

    lax.fori_loop(0, tt, step, 0)


def _wkv_scan(ops, s0):
    t_len, _, lanes = ops[0].shape
    tt = min(t_len, SCAN_T_BLOCK)
    op_spec = pl.BlockSpec((tt, RW_HEAD, LANES), lambda g, j: (j, 0, g))
    st_spec = pl.BlockSpec((RW_HEAD, RW_HEAD, LANES), lambda g, j: (0, 0, g))
    return pl.pallas_call(
        _wkv_scan_kernel,
        out_shape=[jax.ShapeDtypeStruct((t_len, RW_HEAD, lanes), F32),
                   jax.ShapeDtypeStruct((RW_HEAD, RW_HEAD, lanes), F32)],
        grid=(lanes // LANES, t_len // tt),
        in_specs=[op_spec] * 6 + [st_spec],
        out_specs=[op_spec, st_spec],
        compiler_params=_cparams("parallel", "arbitrary"),
        name="wkv_scan",
    )(*ops, s0)


def _wkv_chunk_kernel(r_ref, lw_ref, k_ref, v_ref, a_ref, b_ref, s0_ref, y_ref, s_out_ref, s_scr):
    tm = r_ref.shape[0]
    C = WKV_CHUNK
    n_pairs = D_RW // LANES

    @pl.when(pl.program_id(1) == 0)
    def _():
        s_scr[...] = s0_ref[0]

    ri = lax.broadcasted_iota(jnp.int32, (C, C), 0)
    ci = lax.broadcasted_iota(jnp.int32, (C, C), 1)
    tri = (ci <= ri).astype(BF16)
    r2 = lax.broadcasted_iota(jnp.int32, (2 * C, 2 * C), 0) % C
    c2 = lax.broadcasted_iota(jnp.int32, (2 * C, 2 * C), 1) % C
    strict = c2 < r2
    incl = c2 <= r2
    r4 = lax.broadcasted_iota(jnp.int32, (4 * C, 2 * C), 0)
    c4 = lax.broadcasted_iota(jnp.int32, (4 * C, 2 * C), 1) % C
    strict_incl = (c4 < r4 % C) | ((r4 >= 2 * C) & (c4 == r4 % C))
    eye = (lax.broadcasted_iota(jnp.int32, (2 * C, 2 * C), 0)
           == lax.broadcasted_iota(jnp.int32, (2 * C, 2 * C), 1)).astype(F32)
    head0 = lax.broadcasted_iota(jnp.int32, (1, LANES), 1) < RW_HEAD

    def stack(z):
        return jnp.concatenate([jnp.where(head0, z, 0.0), jnp.where(head0, 0.0, z)], axis=0)

    def dot_t(x, y):
        return lax.dot_general(x, y, (((1,), (1,)), ((), ())), preferred_element_type=F32)

    def chunk_body(c, carry):
        rows = pl.ds(pl.multiple_of(c * C, C), C)
        lw_all = lw_ref[rows, :]
        hi = lw_all.astype(BF16)
        lo = (lw_all - hi.astype(F32)).astype(BF16)
        cum_all = _dot(tri, hi) + _dot(tri, lo)
        pairs = range(n_pairs)
        lane_sl = [slice(p * LANES, (p + 1) * LANES) for p in pairs]
        a_st, r_st, b_st, k_st, v_st, pc = [], [], [], [], [], []
        for p in pairs:
            lanes = lane_sl[p]
            cum = cum_all[:, lanes]
            e_pos = jnp.exp(cum)
            e_neg = jnp.exp(-cum)
            e_prev = jnp.exp(cum - lw_all[:, lanes])
            pc.append(e_pos[C - 1:C, :])
            a_st.append(stack(a_ref[rows, lanes] * e_prev).astype(BF16))
            r_st.append(stack(r_ref[rows, lanes] * e_pos).astype(BF16))
            b_st.append(stack(b_ref[rows, lanes] * e_neg))
            k_st.append(stack(k_ref[rows, lanes] * e_neg))
            v_st.append(stack(v_ref[rows, lanes]).astype(BF16))

        ar = [jnp.concatenate([a_st[p], r_st[p]], axis=0) for p in pairs]
        g = [dot_t(ar[p], jnp.concatenate([b_st[p], k_st[p]], axis=0).astype(BF16))
             for p in pairs]
        n_mat = [jnp.where(strict, g[p][:2 * C, :2 * C], 0.0) for p in pairs]
        m_rb = [jnp.where(incl, g[p][2 * C:, :2 * C], 0.0).astype(BF16) for p in pairs]
        m_akrk = [jnp.where(strict_incl, g[p][:, 2 * C:], 0.0).astype(BF16) for p in pairs]

        t_inv = [eye + n_mat[p] for p in pairs]
        pw = [n_mat[p].astype(BF16) for p in pairs]
        pw = [_dot(pw[p], pw[p]).astype(BF16) for p in pairs]
        for _ in range(WKV_CHUNK_LOG2 - 2):
            both = [_dot(jnp.concatenate([pw[p], t_inv[p].astype(BF16)], axis=0), pw[p]) for p in pairs]
            pw = [both[p][:2 * C].astype(BF16) for p in pairs]
            t_inv = [t_inv[p] + both[p][2 * C:] for p in pairs]
        t_inv = [(t_inv[p] + _dot(t_inv[p].astype(BF16), pw[p])).astype(BF16) for p in pairs]

        s = [s_scr[p] for p in pairs]
        xy = [dot_t(ar[p], s[p].astype(BF16)) + _dot(m_akrk[p], v_st[p]) for p in pairs]
        u = [_dot(t_inv[p], xy[p][:2 * C].astype(BF16)).astype(BF16) for p in pairs]
        for p in pairs:
            y_st = xy[p][2 * C:] + _dot(m_rb[p], u[p])
            y_ref[rows, lane_sl[p]] = y_st[:C] + y_st[C:]
        for p in pairs:
            uv = jnp.concatenate([u[p], v_st[p]], axis=0)
            bk = jnp.concatenate([b_st[p] * pc[p], k_st[p] * pc[p]], axis=0).astype(BF16)
            s_scr[p] = s[p] * pc[p] + lax.dot_general(uv, bk, (((0,), (0,)), ((), ())),
                                                      preferred_element_type=F32)
        return carry

    lax.fori_loop(0, tm // C, chunk_body, 0)
    s_out_ref[0] = s_scr[...]


def _wkv_chunked(ops, s0_pairs, batch, tiles, tm):
    n = ops[0].shape[0]
    n_pairs = D_RW // LANES
    row = pl.BlockSpec((tm, D_RW), lambda b, j: (b * tiles + j, 0))
    st = pl.BlockSpec((1, n_pairs, LANES, LANES), lambda b, j: (b, 0, 0, 0))
    return pl.pallas_call(
        _wkv_chunk_kernel,
        out_shape=[jax.ShapeDtypeStruct((n, D_RW), F32),
                   jax.ShapeDtypeStruct((batch, n_pairs, LANES, LANES), F32)],
        grid=(batch, tiles),
        in_specs=[row] * 6 + [st],
        out_specs=[row, st],
        scratch_shapes=[pltpu.VMEM((n_pairs, LANES, LANES), F32)],
        compiler_params=_cparams("parallel", "arbitrary"),
        name="wkv_chunked",
    )(*ops, s0_pairs)


def _state_to_pairs(s):
    batch = s.shape[0]
    s = s.reshape(batch, RW_HEADS // 2, 2, RW_HEAD, RW_HEAD)
    eye = jnp.eye(2, dtype=s.dtype)
    out = s[:, :, :, :, None, :] * eye[None, None, :, None, :, None]
    return out.reshape(batch, RW_HEADS // 2, LANES, LANES)


def _state_from_pairs(sp):
    batch = sp.shape[0]
    sp = sp.reshape(batch, RW_HEADS // 2, 2, RW_HEAD, 2, RW_HEAD)
    out = jnp.stack([sp[:, :, 0, :, 0, :], sp[:, :, 1, :, 1, :]], axis=2)
    return out.reshape(batch, RW_HEADS, RW_HEAD, RW_HEAD)


def _lru_kernel(G, x_ref, wl_ref, conv0_ref, h0_ref, cw_ref, cb_ref, wgate_ref, brg_ref, big_ref,
                lam_ref, hg_out, conv_out, h_out, hist_ref, hcar_ref, a_scr, u_scr, h_scr):
    hist_steps = CONV_W - 1

    @pl.when(pl.program_id(1) == 0)
    def _():
        hist_ref[...] = conv0_ref[0]
        hcar_ref[...] = h0_ref[0]

    p = _dot(x_ref[...], wl_ref[...])
    n = p.shape[0]
    p_lx = p[:, :D_LRU]
    p_lg = p[:, D_LRU:]
    hist = hist_ref[...]
    cw = cw_ref[...]
    xc = cb_ref[...] + cw[CONV_W - 1:CONV_W, :] * p_lx
    for j in range(hist_steps):
        xc = xc + cw[j:j + 1, :] * _delayed_rows(p_lx, hist, hist_steps - j, G)
    tail = _tail_rows(G, hist_steps)
    hist_ref[...] = p_lx[n - tail:, :]
    conv_out[0] = p_lx[n - tail:, :]

    gates = _dot(xc.astype(BF16), wgate_ref[...])
    rg = jax.nn.sigmoid(gates[:, :D_LRU] + brg_ref[...])
    ig = jax.nn.sigmoid(gates[:, D_LRU:] + big_ref[...])
    log_a = -LRU_C * rg * jax.nn.softplus(-lam_ref[...])
    a = jnp.exp(log_a)
    a_scr[...] = a
    u_scr[...] = jnp.sqrt(-jnp.tanh(log_a) * (a * a + 1.0)) * (ig * xc)

    steps = n // G

    def step(s, h):
        rows = pl.ds(pl.multiple_of(s * G, G), G)
        h = a_scr[rows, :] * h + u_scr[rows, :]
        h_scr[rows, :] = h
        return h
    h = lax.fori_loop(0, steps, step, hcar_ref[...], unroll=min(steps, 8))
    hcar_ref[...] = h
    h_out[0] = h
    hg_out[...] = (h_scr[...] * jax.nn.gelu(p_lg)).astype(hg_out.dtype)


def _lru(x_bf, conv0, h0, wp, l, geom):
    groups, G, tiles, tm = geom
    n = x_bf.shape[0]
    hist_steps = CONV_W - 1
    tail = _tail_rows(G, hist_steps)
    row = lambda c: pl.BlockSpec((tm, c), lambda b, j: (b * tiles + j, 0))
    vec = lambda c: _const_spec((1, c))
    grp = lambda r, c: pl.BlockSpec((1, r, c), lambda b, j: (b, 0, 0))
    return pl.pallas_call(
        functools.partial(_lru_kernel, G),
        out_shape=[jax.ShapeDtypeStruct((n, D_LRU), BF16),
                   jax.ShapeDtypeStruct((groups, tail, D_LRU), F32),
                   jax.ShapeDtypeStruct((groups, G, D_LRU), F32)],
        grid=(groups, tiles),
        in_specs=[row(D_MODEL), _const_spec((D_MODEL, 2 * D_LRU)),
                  grp(tail, D_LRU), grp(G, D_LRU),
                  _const_spec((CONV_W, D_LRU)), vec(D_LRU), _const_spec((D_LRU, 2 * D_LRU)),
                  vec(D_LRU), vec(D_LRU), vec(D_LRU)],
        out_specs=[row(D_LRU), grp(tail, D_LRU), grp(G, D_LRU)],
        scratch_shapes=[pltpu.VMEM((tail, D_LRU), F32), pltpu.VMEM((G, D_LRU), F32),
                        pltpu.VMEM((tm, D_LRU), F32), pltpu.VMEM((tm, D_LRU), F32),
                        pltpu.VMEM((tm, D_LRU), F32)],
        compiler_params=_cparams("parallel", "arbitrary"),
        name="rg_lru",
    )(x_bf, wp['w_lru'][l], conv0, h0, wp['lru_conv_w'][l], wp['lru_conv_b'][l], wp['w_lru_gate'][l],
      wp['lru_b_rg'][l], wp['lru_b_ig'][l], wp['lru_lambda'][l])


def _attn_kernel(q_ref, k_ref, v_ref, o_ref):
    scale = XA_HEAD ** -0.5
    for h in range(XA_HEADS):
        sl = slice(h * XA_HEAD, (h + 1) * XA_HEAD)
        q = q_ref[:, :, sl]
        k = k_ref[:, :, sl].astype(BF16)
        v = v_ref[:, :, sl].astype(BF16)
        s = jnp.einsum('bqd,bkd->bqk', q, k, preferred_element_type=F32) * scale
        s = s - jnp.max(s, axis=-1, keepdims=True)
        e = jnp.exp(s)
        p = (e / jnp.sum(e, axis=-1, keepdims=True)).astype(BF16)
        o = jnp.einsum('bqk,bkd->bqd', p, v, preferred_element_type=F32)
        o_ref[:, :, sl] = o.astype(o_ref.dtype)


def _attention(q3, k3, v3, bb, tq):
    b, t, _ = q3.shape
    qspec = pl.BlockSpec((bb, tq, D_XA), lambda i, j: (i, j, 0))
    kspec = pl.BlockSpec((bb, N_MEM, D_XA), lambda i, j: (i, 0, 0))
    return pl.pallas_call(
        _attn_kernel,
        out_shape=jax.ShapeDtypeStruct((b, t, D_XA), BF16),
        grid=(b // bb, t // tq),
        in_specs=[qspec, kspec, kspec],
        out_specs=qspec,
        compiler_params=_cparams("parallel", "parallel"),
        name="mem_attn",
    )(q3, k3, v3)


def _merge_kernel(x_ref, xbf_ref, wg_ref, y_ref, bonus_ref, g_ref, gng_ref, gnb_ref, hg_ref,
                  oatt_ref, wrw_ref, wlru_ref, wxa_ref, wo_ref, red_ref, exp_ref, lng_ref, lnb_ref,
                  xo_ref, xobf_ref):
    gates = jax.nn.sigmoid(_dot(xbf_ref[...], wg_ref[...]))
    y = y_ref[...]
    inv = 1.0 / RW_HEAD
    d = y - _head_sum(y, red_ref, exp_ref) * inv
    var = _head_sum(d * d, red_ref, exp_ref) * inv
    yn = d * lax.rsqrt(var + GN_EPS) * gng_ref[...] + gnb_ref[...]
    o_rw = _dot(((yn + bonus_ref[...]) * g_ref[...]).astype(BF16), wrw_ref[...])
    o_lru = _dot(hg_ref[...], wlru_ref[...])
    o_xa = _dot(oatt_ref[...], wxa_ref[...])
    m = (gates[:, :D_MODEL] * o_rw + gates[:, D_MODEL:2 * D_MODEL] * o_lru
         + gates[:, 2 * D_MODEL:] * o_xa)
    mix = _dot(m.astype(BF16), wo_ref[...])
    xo = _layer_norm_rows(ALPHA * x_ref[...] + mix, lng_ref[...], lnb_ref[...], LN_EPS)
    xo_ref[...] = xo
    xobf_ref[...] = xo.astype(BF16)


def _merge(x, x_bf, y, bonus, g, hg, oatt, wp, l, tm):
    n = x.shape[0]
    row = lambda c: pl.BlockSpec((tm, c), lambda i: (i, 0))
    vec = lambda c: _const_spec((1, c))
    return pl.pallas_call(
        _merge_kernel,
        out_shape=[jax.ShapeDtypeStruct((n, D_MODEL), F32), jax.ShapeDtypeStruct((n, D_MODEL), BF16)],
        grid=(n // tm,),
        in_specs=[row(D_MODEL), row(D_MODEL), _const_spec((D_MODEL, N_BRANCH * D_MODEL)),
                  row(D_RW), row(D_RW), row(D_RW), vec(D_RW), vec(D_RW), row(D_LRU), row(D_XA),
                  _const_spec((D_RW, D_MODEL)), _const_spec((D_LRU, D_MODEL)),
                  _const_spec((D_XA, D_MODEL)), _const_spec((D_MODEL, D_MODEL)),
                  _const_spec((D_RW, LANES)), _const_spec((LANES, D_RW)), vec(D_MODEL), vec(D_MODEL)],
        out_specs=[row(D_MODEL), row(D_MODEL)],
        compiler_params=_cparams("parallel"),
        name="merge_ln1",
    )(x, x_bf, wp['w_gate'][l], y, bonus, g, wp['rw_gn_g'][l], wp['rw_gn_b'][l], hg, oatt,
      wp['w_rw_out'][l], wp['w_lru_out'][l], wp['w_xa_out'][l], wp['w_o'][l], wp['head_red'],
      wp['head_exp'],
      wp['ln1_g'][l], wp['ln1_b'][l])


def _ffn_kernel(x_ref, xbf_ref, win_ref, wout_ref, lng_ref, lnb_ref, xo_ref, xobf_ref):
    xb = xbf_ref[...]
    acc = None
    for c in range(D_FF // FFN_CHUNK):
        u = _dot(xb, win_ref[:, c * FFN_CHUNK:(c + 1) * FFN_CHUNK])
        gt = _dot(xb, win_ref[:, D_FF + c * FFN_CHUNK:D_FF + (c + 1) * FFN_CHUNK])
        part = _dot((jax.nn.silu(gt) * u).astype(BF16), wout_ref[c * FFN_CHUNK:(c + 1) * FFN_CHUNK, :])
        acc = part if acc is None else acc + part
    xo = _layer_norm_rows(ALPHA * x_ref[...] + acc, lng_ref[...], lnb_ref[...], LN_EPS)
    xo_ref[...] = xo
    xobf_ref[...] = xo.astype(BF16)


def _ffn(x, x_bf, wp, l, tm):
    n = x.shape[0]
    row = lambda c: pl.BlockSpec((tm, c), lambda i: (i, 0))
    vec = lambda c: _const_spec((1, c))
    return pl.pallas_call(
        _ffn_kernel,
        out_shape=[jax.ShapeDtypeStruct((n, D_MODEL), F32), jax.ShapeDtypeStruct((n, D_MODEL), BF16)],
        grid=(n // tm,),
        in_specs=[row(D_MODEL), row(D_MODEL), _const_spec((D_MODEL, 2 * D_FF)),
                  _const_spec((D_FF, D_MODEL)), vec(D_MODEL), vec(D_MODEL)],
        out_specs=[row(D_MODEL), row(D_MODEL)],
        compiler_params=_cparams("parallel"),
        name="ffn_ln2",
    )(x, x_bf, wp['w_ffn_in'][l], wp['w_ffn_out'][l], wp['ln2_g'][l], wp['ln2_b'][l])


def _block_diag(blocks):
    n, bi, bj = blocks.shape
    eye = jnp.eye(n, dtype=blocks.dtype)
    return (eye[:, None, :, None] * blocks[:, :, None, :]).reshape(n * bi, n * bj)


def _prepare_weights(P):
    rowvec = lambda a: a.reshape(a.shape[0], 1, -1).astype(F32)
    w_in = P['w_in']
    c0, c1, c2, c3 = RW_COLS, RW_COLS + D_LRU, RW_COLS + 2 * D_LRU, RW_COLS + 2 * D_LRU + D_XA
    zeros_wa = jnp.zeros((DEPTH, W_LORA, D_RW), F32)
    head_red = (jnp.arange(D_RW)[:, None] // RW_HEAD == jnp.arange(LANES)[None, :]).astype(F32)
    wp = {
        'w_rw': w_in[:, :, :c0].astype(BF16),
        'w_lru': w_in[:, :, c0:c2].astype(BF16),
        'w_q': w_in[:, :, c2:c3].astype(BF16),
        'w_gate': w_in[:, :, c3:].astype(BF16),
        'w_lora': jnp.concatenate(
            [jnp.concatenate([P['rw_w2'], zeros_wa], axis=2),
             jnp.concatenate([zeros_wa, P['rw_a2']], axis=2)], axis=1).astype(BF16),
        'rw_g2': P['rw_g2'].astype(BF16),
        'rw_v1': jnp.pad(P['rw_v1'], ((0, 0), (0, 0), (0, LANES - V_LORA))).astype(BF16),
        'rw_v2': jnp.pad(P['rw_v2'], ((0, 0), (0, LANES - V_LORA), (0, 0))).astype(BF16),
        'rw_rk': rowvec(P['rw_rk'].reshape(DEPTH, D_RW)),
        'head_red': head_red.astype(BF16),
        'head_exp': head_red.T.astype(BF16),
        'w_lru_gate': jnp.concatenate(
            [jax.vmap(_block_diag)(P['lru_w_rg']), jax.vmap(_block_diag)(P['lru_w_ig'])],
            axis=2).astype(BF16),
        'lru_conv_w': P['lru_conv_w'].astype(F32),
        'w_mem_kv': P['w_mem_kv'].astype(BF16),
    }
    for name in ('w_rw_out', 'w_lru_out', 'w_xa_out', 'w_o', 'w_ffn_in', 'w_ffn_out'):
        wp[name] = P[name].astype(BF16)
    for name in ('rw_mu', 'rw_w0', 'rw_a0', 'rw_v0', 'rw_kk', 'rw_ka', 'rw_gn_g', 'rw_gn_b',
                 'lru_conv_b', 'lru_b_rg', 'lru_b_ig', 'lru_lambda', 'ln1_g', 'ln1_b', 'ln2_g', 'ln2_b'):
        wp[name] = rowvec(P[name])
    return wp


def _to_scan_layout(a, batch, t_len, time_major):
    if time_major:
        a = a.reshape(t_len, batch, RW_HEADS, RW_HEAD).transpose(0, 3, 1, 2)
    else:
        a = a.reshape(batch, t_len, RW_HEADS, RW_HEAD).transpose(1, 3, 0, 2)
    a = a.reshape(t_len, RW_HEAD, batch * RW_HEADS)
    pad = -(batch * RW_HEADS) % LANES
    return jnp.pad(a, ((0, 0), (0, 0), (0, pad)))


def _from_scan_layout(y, batch, t_len, time_major):
    y = y[:, :, :batch * RW_HEADS].reshape(t_len, RW_HEAD, batch, RW_HEADS)
    y = y.transpose(0, 2, 3, 1) if time_major else y.transpose(2, 0, 3, 1)
    return y.reshape(batch * t_len, D_RW)


def _state_to_scan_layout(s, batch):
    s = s.transpose(3, 2, 0, 1).reshape(RW_HEAD, RW_HEAD, batch * RW_HEADS)
    pad = -(batch * RW_HEADS) % LANES
    return jnp.pad(s, ((0, 0), (0, 0), (0, pad)))


def _state_from_scan_layout(s, batch):
    s = s[:, :, :batch * RW_HEADS].reshape(RW_HEAD, RW_HEAD, batch, RW_HEADS)
    return s.transpose(2, 3, 1, 0)


def _trunk(x3, mem_k, mem_v, shift, wkv, conv, h, wp, time_major):
    batch, t_len, _ = x3.shape
    assert t_len >= CONV_W - 1
    n = batch * t_len
    hist_steps = CONV_W - 1
    if time_major:
        groups, G, tiles, tm = 1, batch, 1, n
        x = x3.transpose(1, 0, 2).reshape(n, D_MODEL)
        tm_rows = min(n, ROW_TILE)
    else:
        tm = min(t_len, ROW_TILE)
        groups, G, tiles = batch, 1, t_len // tm
        x = x3.reshape(n, D_MODEL)
        tm_rows = tm
    geom = (groups, G, tiles, tm)
    x_bf = x.astype(BF16)
    v_first = None
    n_sh, n_wkv, n_conv, n_h = [], [], [], []
    for l in range(DEPTH):
        if time_major:
            prev = shift[l][None]
            conv0 = conv[l].transpose(1, 0, 2).reshape(1, hist_steps * batch, D_LRU)
            h0 = h[l][None]
        else:
            prev = jnp.pad(shift[l][:, None, :], ((0, 0), (SUBLANES - 1, 0), (0, 0)))
            conv0 = jnp.pad(conv[l], ((0, 0), (SUBLANES - hist_steps, 0), (0, 0)))
            h0 = h[l][:, None, :]
        r, w, k, v, nkk, b, g, bonus, sh_tail = _rw_prep(x_bf, prev, wp, l, geom, v_first,
                                                         log_decay=not time_major)
        if l == 0:
            v_first = v
        if time_major:
            ops = [_to_scan_layout(a, batch, t_len, True) for a in (r, w, k, v, nkk, b)]
            y_s, s_new = _wkv_scan(ops, _state_to_scan_layout(wkv[l], batch))
            y = _from_scan_layout(y_s, batch, t_len, True)
            n_wkv.append(_state_from_scan_layout(s_new, batch))
        else:
            wtm = min(t_len, WKV_ROW_TILE)
            y, s_new = _wkv_chunked((r, w, k, v, nkk, b), _state_to_pairs(wkv[l]), batch,
                                    t_len // wtm, wtm)
            n_wkv.append(_state_from_pairs(s_new))
        n_sh.append(sh_tail[0] if time_major else sh_tail[:, -1, :])
        hg, conv_tail, h_last = _lru(x_bf, conv0, h0, wp, l, geom)
        if time_major:
            n_conv.append(conv_tail[0].reshape(hist_steps, batch, D_LRU).transpose(1, 0, 2))
            n_h.append(h_last[0])
        else:
            n_conv.append(conv_tail[:, SUBLANES - hist_steps:, :])
            n_h.append(h_last[:, 0, :])
        q = _matmul(x_bf, wp['w_q'][l], BF16, "q_proj")
        if time_major:
            q3 = q.reshape(t_len, batch, D_XA).transpose(1, 0, 2)
            t_pad = -t_len % (2 * SUBLANES)
            q3 = jnp.pad(q3, ((0, 0), (0, t_pad), (0, 0)))
            o3 = _attention(q3, mem_k[l], mem_v[l], SUBLANES, t_len + t_pad)
            oatt = o3[:, :t_len].transpose(1, 0, 2).reshape(n, D_XA)
        else:
            q3 = q.reshape(batch, t_len, D_XA)
            oatt = _attention(q3, mem_k[l], mem_v[l], 1, min(t_len, 1024)).reshape(n, D_XA)
        x, x_bf = _merge(x, x_bf, y, bonus, g, hg, oatt, wp, l, tm_rows)
        x, x_bf = _ffn(x, x_bf, wp, l, tm_rows)
    if time_major:
        y3 = x.reshape(t_len, batch, D_MODEL).transpose(1, 0, 2)
    else:
        y3 = x.reshape(batch, t_len, D_MODEL)
    return y3, jnp.stack(n_sh), jnp.stack(n_wkv), jnp.stack(n_conv), jnp.stack(n_h)


def kernel(x_prompt, x_sample, mem_prompt, state_rwkv_shift, state_rwkv_wkv, state_lru_conv,
           state_lru_h, cache_mem_k, cache_mem_v, w_in, rw_mu, rw_w0, rw_w2, rw_a0, rw_a2,
           rw_g2, rw_v0, rw_v1, rw_v2, rw_kk, rw_ka, rw_rk, rw_gn_g, rw_gn_b, w_rw_out,
           lru_conv_w, lru_conv_b, lru_w_rg, lru_b_rg, lru_w_ig, lru_b_ig, lru_lambda, w_lru_out,
           w_mem_kv, w_xa_out, w_o, ln1_g, ln1_b, w_ffn_in, w_ffn_out, ln2_g, ln2_b):
    P = dict(w_in=w_in, rw_mu=rw_mu, rw_w0=rw_w0, rw_w2=rw_w2, rw_a0=rw_a0, rw_a2=rw_a2,
             rw_g2=rw_g2, rw_v0=rw_v0, rw_v1=rw_v1, rw_v2=rw_v2, rw_kk=rw_kk, rw_ka=rw_ka,
             rw_rk=rw_rk, rw_gn_g=rw_gn_g, rw_gn_b=rw_gn_b, w_rw_out=w_rw_out,
             lru_conv_w=lru_conv_w, lru_conv_b=lru_conv_b, lru_w_rg=lru_w_rg, lru_b_rg=lru_b_rg,
             lru_w_ig=lru_w_ig, lru_b_ig=lru_b_ig, lru_lambda=lru_lambda, w_lru_out=w_lru_out,
             w_mem_kv=w_mem_kv, w_xa_out=w_xa_out, w_o=w_o, ln1_g=ln1_g, ln1_b=ln1_b,
             w_ffn_in=w_ffn_in, w_ffn_out=w_ffn_out, ln2_g=ln2_g, ln2_b=ln2_b)
    wp = _prepare_weights(P)

    bp, n_mem, _ = mem_prompt.shape
    mem_bf = mem_prompt.reshape(bp * n_mem, D_MODEL).astype(BF16)
    mk, mv = [], []
    for l in range(DEPTH):
        kv = _matmul(mem_bf, wp['w_mem_kv'][l], F32, "mem_kv")
        mk.append(kv[:, :D_XA].reshape(bp, n_mem, D_XA))
        mv.append(kv[:, D_XA:].reshape(bp, n_mem, D_XA))
    new_mem_k = jnp.stack(mk).reshape(DEPTH, bp, n_mem, XA_HEADS, XA_HEAD)
    new_mem_v = jnp.stack(mv).reshape(DEPTH, bp, n_mem, XA_HEADS, XA_HEAD)
    dt = x_prompt.dtype
    z_shift = jnp.zeros((DEPTH, bp, RW_COLS), dt)
    z_wkv = jnp.zeros((DEPTH, bp, RW_HEADS, RW_HEAD, RW_HEAD), dt)
    z_conv = jnp.zeros((DEPTH, bp, CONV_W - 1, D_LRU), dt)
    z_h = jnp.zeros((DEPTH, bp, D_LRU), dt)
    y_prompt, sh_p, wkv_p, conv_p, h_p = _trunk(
        x_prompt, mk, mv, z_shift, z_wkv, z_conv, z_h, wp, time_major=False)

    bs = x_sample.shape[0]
    ck = cache_mem_k.reshape(DEPTH, bs, n_mem, D_XA)
    cv = cache_mem_v.reshape(DEPTH, bs, n_mem, D_XA)
    y_sample, sh_s, wkv_s, conv_s, h_s = _trunk(
        x_sample, ck, cv, state_rwkv_shift, state_rwkv_wkv, state_lru_conv, state_lru_h, wp,
        time_major=True)
    return (y_prompt, y_sample, sh_p, wkv_p, conv_p, h_p, new_mem_k, new_mem_v,
            sh_s, wkv_s, conv_s, h_s)
```

```python
import functools

import jax
import jax.numpy as jnp
from jax import lax
from jax.experimental import pallas as pl
from jax.experimental.pallas import tpu as pltpu

F32 = jnp.float32
BF16 = jnp.bfloat16

D_MODEL = 1024
DEPTH = 4
N_MEM = 256
RW_HEADS = 12
RW_HEAD = 64
D_RW = RW_HEADS * RW_HEAD
W_LORA = 64
A_LORA = 64
V_LORA = 32
G_LORA = 128
RW_COLS = 3 * D_RW + W_LORA + A_LORA + G_LORA
GN_EPS = 64e-5
LRU_BLOCKS = 12
LRU_BW = 64
D_LRU = LRU_BLOCKS * LRU_BW
CONV_W = 4
LRU_C = 8.0
XA_HEADS = 4
XA_HEAD = 128
D_XA = XA_HEADS * XA_HEAD
N_BRANCH = 3
D_FF = 2816
ALPHA = (2 * DEPTH) ** 0.25
LN_EPS = 1e-5

LANES = 128
SUBLANES = 8
VMEM_LIMIT_BYTES = 52 * 1024 * 1024

ROW_TILE = 256
MERGE_ROW_TILE = 256
FFN_ROW_TILE = 512
FFN_CHUNK = 1408
SCAN_T_BLOCK = 32
SCAN_K_UNROLL = 4
WKV_CHUNK_LOG2 = 6
WKV_CHUNK = 1 << WKV_CHUNK_LOG2
WKV_ROW_TILE = 128
WKV_SEQS_PER_STEP = 4
ATTN_CACHE_BATCH = 8
ATTN_MASKED = -1e30


def _cparams(*sem):
    return pltpu.CompilerParams(dimension_semantics=sem, vmem_limit_bytes=VMEM_LIMIT_BYTES)


def _const_spec(shape):
    nd = len(shape)
    return pl.BlockSpec(shape, lambda *_: (0,) * nd, pipeline_mode=pl.Buffered(1))


def _dot(a, b):
    return jnp.dot(a, b, preferred_element_type=F32)


def _split_bf16(x):
    hi = x.astype(BF16)
    return hi, (x - hi.astype(F32)).astype(BF16)


def _head_sum(x, red_ref, exp_ref):
    hi, lo = _split_bf16(x)
    red = red_ref[...]
    s_hi, s_lo = _split_bf16(_dot(hi, red) + _dot(lo, red))
    ex = exp_ref[...]
    return _dot(s_hi, ex) + _dot(s_lo, ex)


def _layer_norm_rows(x, g, b, eps):
    mu = jnp.mean(x, axis=-1, keepdims=True)
    d = x - mu
    var = jnp.mean(d * d, axis=-1, keepdims=True)
    return d * lax.rsqrt(var + eps) * g + b


def _delayed_rows(p, hist, j, G):
    n = p.shape[0]
    h = hist.shape[0] // G
    if G % SUBLANES == 0:
        ext = jnp.concatenate([hist, p], axis=0)
        return ext[(h - j) * G:(h - j) * G + n]
    assert G == 1
    out = pltpu.roll(p, j, 0)
    row = lax.broadcasted_iota(jnp.int32, (n, 1), 0)
    for i in range(j):
        out = jnp.where(row == i, hist[h - j + i:h - j + i + 1, :], out)
    return out


def _tail_rows(G, steps):
    return SUBLANES if G == 1 else steps * G


def _mm_kernel(x_ref, w_ref, o_ref):
    o_ref[...] = _dot(x_ref[...], w_ref[...]).astype(o_ref.dtype)


def _matmul(x_bf, w_bf, out_dtype, name):
    n, k = x_bf.shape
    m = w_bf.shape[1]
    tm = min(n, 512)
    return pl.pallas_call(
        _mm_kernel,
        out_shape=jax.ShapeDtypeStruct((n, m), out_dtype),
        grid=(n // tm,),
        in_specs=[pl.BlockSpec((tm, k), lambda i: (i, 0)), _const_spec((k, m))],
        out_specs=pl.BlockSpec((tm, m), lambda i: (i, 0)),
        compiler_params=_cparams("parallel"),
        name=name,
    )(x_bf, w_bf)


def _rw_prep_kernel(first_layer, G, log_decay, *refs):
    if first_layer:
        (x_ref, wrw_ref, prev_ref, mu_ref, wlora_ref, w0_ref, a0_ref, g2_ref,
         kkp_ref, kap_ref, rk_ref, red_ref, exp_ref,
         r_out, w_out, k_out, v_out, nkk_out, b_out, g_out, bonus_out, tail_out,
         carry_ref) = refs
    else:
        (x_ref, wrw_ref, prev_ref, mu_ref, wlora_ref, w0_ref, a0_ref, g2_ref,
         kkp_ref, kap_ref, rk_ref, red_ref, exp_ref, vfirst_ref, v0_ref, v1_ref, v2_ref,
         r_out, w_out, k_out, v_out, nkk_out, b_out, g_out, bonus_out, tail_out,
         carry_ref) = refs

    @pl.when(pl.program_id(1) == 0)
    def _():
        carry_ref[...] = prev_ref[0]

    p = _dot(x_ref[...], wrw_ref[...])
    n = p.shape[0]
    p_prev = _delayed_rows(p, carry_ref[...], 1, G)
    tail = _tail_rows(G, 1)
    carry_ref[...] = p[n - tail:, :]
    tail_out[0] = p[n - tail:, :]
    xs = p + (p_prev - p) * mu_ref[...]

    r = xs[:, 0:D_RW]
    k = xs[:, D_RW:2 * D_RW]
    v = xs[:, 2 * D_RW:3 * D_RW]
    z = xs[:, 3 * D_RW:3 * D_RW + W_LORA + A_LORA]
    gd = xs[:, 3 * D_RW + W_LORA + A_LORA:]

    lane = lax.broadcasted_iota(jnp.int32, (1, W_LORA + A_LORA), 1)
    zt = jnp.where(lane < W_LORA, jnp.tanh(z), z)
    lo = _dot(zt.astype(BF16), wlora_ref[...])
    w = -jax.nn.softplus(-(w0_ref[...] + lo[:, :D_RW])) - 0.5
    decay = -jnp.exp(w) if log_decay else jnp.exp(-jnp.exp(w))
    a = jax.nn.sigmoid(a0_ref[...] + lo[:, D_RW:])
    g = _dot(jax.nn.sigmoid(gd).astype(BF16), g2_ref[...])

    if not first_layer:
        vv = _dot(_dot(v.astype(BF16), v1_ref[...]).astype(BF16), v2_ref[...])
        v = v + (vfirst_ref[...] - v) * jax.nn.sigmoid(v0_ref[...] + vv)

    kk = k * kkp_ref[...]
    kk = kk / jnp.maximum(jnp.sqrt(_head_sum(kk * kk, red_ref, exp_ref)), 1e-12)
    k2 = k * (1.0 + (a - 1.0) * kap_ref[...])
    bonus = _head_sum(r * k2 * rk_ref[...], red_ref, exp_ref) * v

    r_out[...] = r
    w_out[...] = decay
    k_out[...] = k2
    v_out[...] = v
    nkk_out[...] = -kk
    b_out[...] = kk * a
    g_out[...] = g
    bonus_out[...] = bonus


def _rw_prep(x_bf, prev, wp, l, geom, v_first, log_decay):
    groups, G, tiles, tm = geom
    n = x_bf.shape[0]
    first = v_first is None
    row = lambda c: pl.BlockSpec((tm, c), lambda b, j: (b * tiles + j, 0))
    vec = lambda c: _const_spec((1, c))
    tail = _tail_rows(G, 1)
    in_specs = [row(D_MODEL), _const_spec((D_MODEL, RW_COLS)),
                pl.BlockSpec((1, tail, RW_COLS), lambda b, j: (b, 0, 0)),
                vec(RW_COLS), _const_spec((W_LORA + A_LORA, 2 * D_RW)), vec(D_RW), vec(D_RW),
                _const_spec((G_LORA, D_RW)), vec(D_RW), vec(D_RW), vec(D_RW),
                _const_spec((D_RW, LANES)), _const_spec((LANES, D_RW))]
    args = [x_bf, wp['w_rw'][l], prev, wp['rw_mu'][l], wp['w_lora'][l], wp['rw_w0'][l],
            wp['rw_a0'][l], wp['rw_g2'][l], wp['rw_kk'][l], wp['rw_ka'][l], wp['rw_rk'][l],
            wp['head_red'], wp['head_exp']]
    if not first:
        in_specs += [row(D_RW), vec(D_RW), _const_spec((D_RW, LANES)), _const_spec((LANES, D_RW))]
        args += [v_first, wp['rw_v0'][l - 1], wp['rw_v1'][l - 1], wp['rw_v2'][l - 1]]
    out_shape = [jax.ShapeDtypeStruct((n, D_RW), F32)] * 8 + [
        jax.ShapeDtypeStruct((groups, tail, RW_COLS), F32)]
    out_specs = [row(D_RW)] * 8 + [pl.BlockSpec((1, tail, RW_COLS), lambda b, j: (b, 0, 0))]
    return pl.pallas_call(
        functools.partial(_rw_prep_kernel, first, G, log_decay),
        out_shape=out_shape,
        grid=(groups, tiles),
        in_specs=in_specs,
        out_specs=out_specs,
        scratch_shapes=[pltpu.VMEM((tail, RW_COLS), F32)],
        compiler_params=_cparams("parallel", "arbitrary"),
        name="rw_prep",
    )(*args)


def _wkv_scan_kernel(r_ref, w_ref, k_ref, v_ref, nkk_ref, b_ref, s0_ref, y_ref, s_out_ref):
    tt = r_ref.shape[0]
    vt = RW_HEAD // SUBLANES

    @pl.when(pl.program_id(1) == 0)
    def _():
        s_out_ref[...] = s0_ref[...]

    def bcast(ref, t, kidx):
        return jnp.broadcast_to(ref[t, pl.ds(kidx, 1), :], (SUBLANES, LANES))

    def step(t, carry):
        def sa_body(kidx, sa):
            nk = bcast(nkk_ref, t, kidx)
            return tuple(sa[i] + s_out_ref[kidx, i * SUBLANES:(i + 1) * SUBLANES, :] * nk
                         for i in range(vt))
        zeros = tuple(jnp.zeros((SUBLANES, LANES), F32) for _ in range(vt))
        sa = lax.fori_loop(0, RW_HEAD, sa_body, zeros, unroll=SCAN_K_UNROLL)
        vv = tuple(v_ref[t, i * SUBLANES:(i + 1) * SUBLANES, :] for i in range(vt))

        def up_body(kidx, y):
            wk = bcast(w_ref, t, kidx)
            bk = bcast(b_ref, t, kidx)
            kk = bcast(k_ref, t, kidx)
            rk = bcast(r_ref, t, kidx)
            out = []
            for i in range(vt):
                sl = slice(i * SUBLANES, (i + 1) * SUBLANES)
                s_new = s_out_ref[kidx, sl, :] * wk + sa[i] * bk + vv[i] * kk
                s_out_ref[kidx, sl, :] = s_new
                out.append(y[i] + s_new * rk)
            return tuple(out)
        y = lax.fori_loop(0, RW_HEAD, up_body, zeros, unroll=SCAN_K_UNROLL)
        for i in range(vt):
            y_ref[t, i * SUBLANES:(i + 1) * SUBLANES, :] = y[i]
        return carry

    lax.fori_loop(0, tt, step, 0)


def _wkv_scan(ops, s0):
    t_len, _, lanes = ops[0].shape
    tt = min(t_len, SCAN_T_BLOCK)
    op_spec = pl.BlockSpec((tt, RW_HEAD, LANES), lambda g, j: (j, 0, g))
    st_spec = pl.BlockSpec((RW_HEAD, RW_HEAD, LANES), lambda g, j: (0, 0, g))
    return pl.pallas_call(
        _wkv_scan_kernel,
        out_shape=[jax.ShapeDtypeStruct((t_len, RW_HEAD, lanes), F32),
                   jax.ShapeDtypeStruct((RW_HEAD, RW_HEAD, lanes), F32)],
        grid=(lanes // LANES, t_len // tt),
        in_specs=[op_spec] * 6 + [st_spec],
        out_specs=[op_spec, st_spec],
        compiler_params=_cparams("parallel", "arbitrary"),
        name="wkv_scan",
    )(*ops, s0)


def _wkv_chunk_kernel(r_ref, lw_ref, k_ref, v_ref, a_ref, b_ref, s0_ref, y_ref, s_out_ref, s_scr):
    nb, tm, _ = r_ref.shape
    C = WKV_CHUNK
    n_pairs = D_RW // LANES

    @pl.when(pl.program_id(1) == 0)
    def _():
        s_scr[...] = s0_ref[...]

    ri = lax.broadcasted_iota(jnp.int32, (C, C), 0)
    ci = lax.broadcasted_iota(jnp.int32, (C, C), 1)
    tri = (ci <= ri).astype(BF16)
    r2 = lax.broadcasted_iota(jnp.int32, (2 * C, 2 * C), 0) % C
    c2 = lax.broadcasted_iota(jnp.int32, (2 * C, 2 * C), 1) % C
    strict = c2 < r2
    incl = c2 <= r2
    r4 = lax.broadcasted_iota(jnp.int32, (4 * C, 2 * C), 0)
    c4 = lax.broadcasted_iota(jnp.int32, (4 * C, 2 * C), 1) % C
    strict_incl = (c4 < r4 % C) | ((r4 >= 2 * C) & (c4 == r4 % C))
    eye = (lax.broadcasted_iota(jnp.int32, (2 * C, 2 * C), 0)
           == lax.broadcasted_iota(jnp.int32, (2 * C, 2 * C), 1)).astype(F32)
    head0 = lax.broadcasted_iota(jnp.int32, (1, LANES), 1) < RW_HEAD

    def stack(z):
        return jnp.concatenate([jnp.where(head0, z, 0.0), jnp.where(head0, 0.0, z)], axis=0)

    def dot_t(x, y):
        return lax.dot_general(x, y, (((1,), (1,)), ((), ())), preferred_element_type=F32)

    def chunk_body(c, carry):
        rows = pl.ds(pl.multiple_of(c * C, C), C)
        units = [(i, p) for i in range(nb) for p in range(n_pairs)]
        pairs = range(len(units))
        a_st, r_st, b_st, k_st, v_st, pc = [], [], [], [], [], []
        for i in range(nb):
            lw_all = lw_ref[i, rows, :]
            hi, lo = _split_bf16(lw_all)
            cum_all = _dot(tri, hi) + _dot(tri, lo)
            for p in range(n_pairs):
                lanes = slice(p * LANES, (p + 1) * LANES)
                cum = cum_all[:, lanes]
                e_pos = jnp.exp(cum)
                e_neg = jnp.exp(-cum)
                e_prev = jnp.exp(cum - lw_all[:, lanes])
                pc.append(e_pos[C - 1:C, :])
                a_st.append(stack(a_ref[i, rows, lanes] * e_prev).astype(BF16))
                r_st.append(stack(r_ref[i, rows, lanes] * e_pos).astype(BF16))
                b_st.append(stack(b_ref[i, rows, lanes] * e_neg))
                k_st.append(stack(k_ref[i, rows, lanes] * e_neg))
                v_st.append(stack(v_ref[i, rows, lanes]).astype(BF16))

        ar = [jnp.concatenate([a_st[p], r_st[p]], axis=0) for p in pairs]
        g = [dot_t(ar[p], jnp.concatenate([b_st[p], k_st[p]], axis=0).astype(BF16))
             for p in pairs]
        n_mat = [jnp.where(strict, g[p][:2 * C, :2 * C], 0.0) for p in pairs]
        m_rb = [jnp.where(incl, g[p][2 * C:, :2 * C], 0.0).astype(BF16) for p in pairs]
        m_akrk = [jnp.where(strict_incl, g[p][:, 2 * C:], 0.0).astype(BF16) for p in pairs]

        t_inv = [eye + n_mat[p] for p in pairs]
        pw = [n_mat[p].astype(BF16) for p in pairs]
        pw = [_dot(pw[p], pw[p]).astype(BF16) for p in pairs]
        for _ in range(WKV_CHUNK_LOG2 - 2):
            both = [_dot(jnp.concatenate([pw[p], t_inv[p].astype(BF16)], axis=0), pw[p]) for p in pairs]
            pw = [both[p][:2 * C].astype(BF16) for p in pairs]
            t_inv = [t_inv[p] + both[p][2 * C:] for p in pairs]
        t_inv = [(t_inv[p] + _dot(t_inv[p].astype(BF16), pw[p])).astype(BF16) for p in pairs]

        s = [s_scr[i, p] for i, p in units]
        xy = [dot_t(ar[p], s[p].astype(BF16)) + _dot(m_akrk[p], v_st[p]) for p in pairs]
        u = [_dot(t_inv[p], xy[p][:2 * C].astype(BF16)).astype(BF16) for p in pairs]
        for p in pairs:
            i, hp = units[p]
            y_st = xy[p][2 * C:] + _dot(m_rb[p], u[p])
            y_ref[i, rows, hp * LANES:(hp + 1) * LANES] = y_st[:C] + y_st[C:]
        for p in pairs:
            i, hp = units[p]
            uv = jnp.concatenate([u[p], v_st[p]], axis=0)
            bk = jnp.concatenate([b_st[p] * pc[p], k_st[p] * pc[p]], axis=0).astype(BF16)
            s_scr[i, hp] = s[p] * pc[p] + lax.dot_general(uv, bk, (((0,), (0,)), ((), ())),
                                                          preferred_element_type=F32)
        return carry

    lax.fori_loop(0, tm // C, chunk_body, 0)
    s_out_ref[...] = s_scr[...]


def _wkv_chunked(ops, s0_pairs, batch, t_len):
    n_pairs = D_RW // LANES
    nb = WKV_SEQS_PER_STEP if batch % WKV_SEQS_PER_STEP == 0 else 1
    tm = min(t_len, WKV_ROW_TILE)
    ops = [a.reshape(batch, t_len, D_RW) for a in ops]
    row = pl.BlockSpec((nb, tm, D_RW), lambda b, j: (b, j, 0))
    st = pl.BlockSpec((nb, n_pairs, LANES, LANES), lambda b, j: (b, 0, 0, 0))
    y, s_new = pl.pallas_call(
        _wkv_chunk_kernel,
        out_shape=[jax.ShapeDtypeStruct((batch, t_len, D_RW), F32),
                   jax.ShapeDtypeStruct((batch, n_pairs, LANES, LANES), F32)],
        grid=(batch // nb, t_len // tm),
        in_specs=[row] * 6 + [st],
        out_specs=[row, st],
        scratch_shapes=[pltpu.VMEM((nb, n_pairs, LANES, LANES), F32)],
        compiler_params=_cparams("parallel", "arbitrary"),
        name="wkv_chunked",
    )(*ops, s0_pairs)
    return y.reshape(batch * t_len, D_RW), s_new


def _state_to_pairs(s):
    batch = s.shape[0]
    s = s.reshape(batch, RW_HEADS // 2, 2, RW_HEAD, RW_HEAD)
    eye = jnp.eye(2, dtype=s.dtype)
    out = s[:, :, :, :, None, :] * eye[None, None, :, None, :, None]
    return out.reshape(batch, RW_HEADS // 2, LANES, LANES)


def _state_from_pairs(sp):
    batch = sp.shape[0]
    sp = sp.reshape(batch, RW_HEADS // 2, 2, RW_HEAD, 2, RW_HEAD)
    out = jnp.stack([sp[:, :, 0, :, 0, :], sp[:, :, 1, :, 1, :]], axis=2)
    return out.reshape(batch, RW_HEADS, RW_HEAD, RW_HEAD)


def _lru_kernel(G, x_ref, wl_ref, conv0_ref, h0_ref, cw_ref, cb_ref, wgate_ref, brg_ref, big_ref,
                lam_ref, hg_out, conv_out, h_out, hist_ref, hcar_ref, a_scr, u_scr, h_scr):
    hist_steps = CONV_W - 1

    @pl.when(pl.program_id(1) == 0)
    def _():
        hist_ref[...] = conv0_ref[0]
        hcar_ref[...] = h0_ref[0]

    p = _dot(x_ref[...], wl_ref[...])
    n = p.shape[0]
    p_lx = p[:, :D_LRU]
    p_lg = p[:, D_LRU:]
    hist = hist_ref[...]
    cw = cw_ref[...]
    xc = cb_ref[...] + cw[CONV_W - 1:CONV_W, :] * p_lx
    for j in range(hist_steps):
        xc = xc + cw[j:j + 1, :] * _delayed_rows(p_lx, hist, hist_steps - j, G)
    tail = _tail_rows(G, hist_steps)
    hist_ref[...] = p_lx[n - tail:, :]
    conv_out[0] = p_lx[n - tail:, :]

    gates = _dot(xc.astype(BF16), wgate_ref[...])
    rg = jax.nn.sigmoid(gates[:, :D_LRU] + brg_ref[...])
    ig = jax.nn.sigmoid(gates[:, D_LRU:] + big_ref[...])
    log_a = -LRU_C * rg * jax.nn.softplus(-lam_ref[...])
    a = jnp.exp(log_a)
    a_scr[...] = a
    u_scr[...] = jnp.sqrt(-jnp.tanh(log_a) * (a * a + 1.0)) * (ig * xc)

    steps = n // G

    if G == 1:
        sub = lax.broadcasted_iota(jnp.int32, (SUBLANES, 1), 0)

        def step(s, h):
            rows = pl.ds(pl.multiple_of(s * SUBLANES, SUBLANES), SUBLANES)
            a_t = a_scr[rows, :]
            u_t = u_scr[rows, :]
            for d in (1, 2, 4):
                a_prev = jnp.where(sub >= d, pltpu.roll(a_t, d, 0), 1.0)
                u_prev = jnp.where(sub >= d, pltpu.roll(u_t, d, 0), 0.0)
                u_t = a_t * u_prev + u_t
                a_t = a_t * a_prev
            h_rows = a_t * h + u_t
            h_scr[rows, :] = h_rows
            return h_rows[SUBLANES - 1:, :]
        h = lax.fori_loop(0, steps // SUBLANES, step, hcar_ref[...], unroll=4)
    else:
        def step(s, h):
            rows = pl.ds(pl.multiple_of(s * G, G), G)
            h = a_scr[rows, :] * h + u_scr[rows, :]
            h_scr[rows, :] = h
            return h
        h = lax.fori_loop(0, steps, step, hcar_ref[...], unroll=True)
    hcar_ref[...] = h
    h_out[0] = h
    hg_out[...] = (h_scr[...] * jax.nn.gelu(p_lg)).astype(hg_out.dtype)


def _lru(x_bf, conv0, h0, wp, l, geom):
    groups, G, tiles, tm = geom
    n = x_bf.shape[0]
    hist_steps = CONV_W - 1
    tail = _tail_rows(G, hist_steps)
    row = lambda c: pl.BlockSpec((tm, c), lambda b, j: (b * tiles + j, 0))
    vec = lambda c: _const_spec((1, c))
    grp = lambda r, c: pl.BlockSpec((1, r, c), lambda b, j: (b, 0, 0))
    return pl.pallas_call(
        functools.partial(_lru_kernel, G),
        out_shape=[jax.ShapeDtypeStruct((n, D_LRU), BF16),
                   jax.ShapeDtypeStruct((groups, tail, D_LRU), F32),
                   jax.ShapeDtypeStruct((groups, G, D_LRU), F32)],
        grid=(groups, tiles),
        in_specs=[row(D_MODEL), _const_spec((D_MODEL, 2 * D_LRU)),
                  grp(tail, D_LRU), grp(G, D_LRU),
                  _const_spec((CONV_W, D_LRU)), vec(D_LRU), _const_spec((D_LRU, 2 * D_LRU)),
                  vec(D_LRU), vec(D_LRU), vec(D_LRU)],
        out_specs=[row(D_LRU), grp(tail, D_LRU), grp(G, D_LRU)],
        scratch_shapes=[pltpu.VMEM((tail, D_LRU), F32), pltpu.VMEM((G, D_LRU), F32),
                        pltpu.VMEM((tm, D_LRU), F32), pltpu.VMEM((tm, D_LRU), F32),
                        pltpu.VMEM((tm, D_LRU), F32)],
        compiler_params=_cparams("parallel", "arbitrary"),
        name="rg_lru",
    )(x_bf, wp['w_lru'][l], conv0, h0, wp['lru_conv_w'][l], wp['lru_conv_b'][l], wp['w_lru_gate'][l],
      wp['lru_b_rg'][l], wp['lru_b_ig'][l], wp['lru_lambda'][l])


def _attn_heads(q_ref, o_ref, key_of, val_of):
    scale = XA_HEAD ** -0.5
    for h in range(XA_HEADS):
        sl = slice(h * XA_HEAD, (h + 1) * XA_HEAD)
        q = q_ref[:, :, sl]
        k = key_of(h).astype(BF16)
        v = val_of(h).astype(BF16)
        s = jnp.einsum('bqd,bkd->bqk', q, k, preferred_element_type=F32) * scale
        s = s - jnp.max(s, axis=-1, keepdims=True)
        e = jnp.exp(s)
        p = (e / jnp.sum(e, axis=-1, keepdims=True)).astype(BF16)
        o = jnp.einsum('bqk,bkd->bqd', p, v, preferred_element_type=F32)
        o_ref[:, :, sl] = o.astype(o_ref.dtype)


def _attn_packed_kernel(q_ref, kv_ref, o_ref):
    _attn_heads(q_ref, o_ref,
                lambda h: kv_ref[:, :, h * XA_HEAD:(h + 1) * XA_HEAD],
                lambda h: kv_ref[:, :, D_XA + h * XA_HEAD:D_XA + (h + 1) * XA_HEAD])


def _attn_cache_kernel(q_ref, k_ref, v_ref, o_ref):
    bb, tq, _ = q_ref.shape
    n_rows = XA_HEADS * tq
    n_cols = N_MEM * XA_HEADS
    own = (lax.broadcasted_iota(jnp.int32, (n_rows, n_cols), 1) % XA_HEADS
           == lax.broadcasted_iota(jnp.int32, (n_rows, n_cols), 0) // tq)
    scale = XA_HEAD ** -0.5
    for i in range(bb):
        q = jnp.concatenate([q_ref[i, :, h * XA_HEAD:(h + 1) * XA_HEAD] for h in range(XA_HEADS)], axis=0)
        k = k_ref[i].astype(BF16)
        v = v_ref[i].astype(BF16)
        s = lax.dot_general(q, k, (((1,), (1,)), ((), ())), preferred_element_type=F32) * scale
        s = jnp.where(own, s, ATTN_MASKED)
        s = s - jnp.max(s, axis=-1, keepdims=True)
        e = jnp.exp(s)
        p = (e / jnp.sum(e, axis=-1, keepdims=True)).astype(BF16)
        o = _dot(p, v)
        for h in range(XA_HEADS):
            o_ref[i, :, h * XA_HEAD:(h + 1) * XA_HEAD] = o[h * tq:(h + 1) * tq].astype(o_ref.dtype)


def _attention_packed(q3, kv3, tq):
    b, t, _ = q3.shape
    qspec = pl.BlockSpec((1, tq, D_XA), lambda i, j: (i, j, 0))
    return pl.pallas_call(
        _attn_packed_kernel,
        out_shape=jax.ShapeDtypeStruct((b, t, D_XA), BF16),
        grid=(b, t // tq),
        in_specs=[qspec, pl.BlockSpec((1, N_MEM, 2 * D_XA), lambda i, j: (i, 0, 0))],
        out_specs=qspec,
        compiler_params=_cparams("parallel", "parallel"),
        name="mem_attn",
    )(q3, kv3)


def _attention_cache(q3, cache_k, cache_v, l, bb):
    b, t, _ = q3.shape
    depth = cache_k.shape[0]
    cache_k = cache_k.reshape(depth, b, N_MEM * XA_HEADS, XA_HEAD)
    cache_v = cache_v.reshape(depth, b, N_MEM * XA_HEADS, XA_HEAD)
    qspec = pl.BlockSpec((bb, t, D_XA), lambda i: (i, 0, 0))
    cspec = pl.BlockSpec((None, bb, N_MEM * XA_HEADS, XA_HEAD), lambda i: (l, i, 0, 0))
    return pl.pallas_call(
        _attn_cache_kernel,
        out_shape=jax.ShapeDtypeStruct((b, t, D_XA), BF16),
        grid=(b // bb,),
        in_specs=[qspec, cspec, cspec],
        out_specs=qspec,
        compiler_params=_cparams("parallel"),
        name="mem_attn_cache",
    )(q3, cache_k, cache_v)


def _merge_kernel(x_ref, xbf_ref, wg_ref, y_ref, bonus_ref, g_ref, gng_ref, gnb_ref, hg_ref,
                  oatt_ref, wrw_ref, wlru_ref, wxa_ref, wo_ref, red_ref, exp_ref, lng_ref, lnb_ref,
                  xo_ref, xobf_ref):
    gates = jax.nn.sigmoid(_dot(xbf_ref[...], wg_ref[...]))
    y = y_ref[...]
    inv = 1.0 / RW_HEAD
    d = y - _head_sum(y, red_ref, exp_ref) * inv
    var = _head_sum(d * d, red_ref, exp_ref) * inv
    yn = d * lax.rsqrt(var + GN_EPS) * gng_ref[...] + gnb_ref[...]
    o_rw = _dot(((yn + bonus_ref[...]) * g_ref[...]).astype(BF16), wrw_ref[...])
    o_lru = _dot(hg_ref[...], wlru_ref[...])
    o_xa = _dot(oatt_ref[...], wxa_ref[...])
    m = (gates[:, :D_MODEL] * o_rw + gates[:, D_MODEL:2 * D_MODEL] * o_lru
         + gates[:, 2 * D_MODEL:] * o_xa)
    mix = _dot(m.astype(BF16), wo_ref[...])
    xo = _layer_norm_rows(ALPHA * x_ref[...] + mix, lng_ref[...], lnb_ref[...], LN_EPS)
    xo_ref[...] = xo
    xobf_ref[...] = xo.astype(BF16)


def _merge(x, x_bf, y, bonus, g, hg, oatt, wp, l, tm):
    n = x.shape[0]
    row = lambda c: pl.BlockSpec((tm, c), lambda i: (i, 0))
    vec = lambda c: _const_spec((1, c))
    return pl.pallas_call(
        _merge_kernel,
        out_shape=[jax.ShapeDtypeStruct((n, D_MODEL), F32), jax.ShapeDtypeStruct((n, D_MODEL), BF16)],
        grid=(n // tm,),
        in_specs=[row(D_MODEL), row(D_MODEL), _const_spec((D_MODEL, N_BRANCH * D_MODEL)),
                  row(D_RW), row(D_RW), row(D_RW), vec(D_RW), vec(D_RW), row(D_LRU), row(D_XA),
                  _const_spec((D_RW, D_MODEL)), _const_spec((D_LRU, D_MODEL)),
                  _const_spec((D_XA, D_MODEL)), _const_spec((D_MODEL, D_MODEL)),
                  _const_spec((D_RW, LANES)), _const_spec((LANES, D_RW)), vec(D_MODEL), vec(D_MODEL)],
        out_specs=[row(D_MODEL), row(D_MODEL)],
        compiler_params=_cparams("parallel"),
        name="merge_ln1",
    )(x, x_bf, wp['w_gate'][l], y, bonus, g, wp['rw_gn_g'][l], wp['rw_gn_b'][l], hg, oatt,
      wp['w_rw_out'][l], wp['w_lru_out'][l], wp['w_xa_out'][l], wp['w_o'][l], wp['head_red'],
      wp['head_exp'],
      wp['ln1_g'][l], wp['ln1_b'][l])


def _ffn_kernel(x_ref, xbf_ref, win_ref, wout_ref, lng_ref, lnb_ref, xo_ref, xobf_ref):
    xb = xbf_ref[...]
    acc = None
    for c in range(D_FF // FFN_CHUNK):
        u = _dot(xb, win_ref[:, c * FFN_CHUNK:(c + 1) * FFN_CHUNK])
        gt = _dot(xb, win_ref[:, D_FF + c * FFN_CHUNK:D_FF + (c + 1) * FFN_CHUNK])
        part = _dot((jax.nn.silu(gt) * u).astype(BF16), wout_ref[c * FFN_CHUNK:(c + 1) * FFN_CHUNK, :])
        acc = part if acc is None else acc + part
    xo = _layer_norm_rows(ALPHA * x_ref[...] + acc, lng_ref[...], lnb_ref[...], LN_EPS)
    xo_ref[...] = xo
    xobf_ref[...] = xo.astype(BF16)


def _ffn(x, x_bf, wp, l, tm):
    n = x.shape[0]
    row = lambda c: pl.BlockSpec((tm, c), lambda i: (i, 0))
    vec = lambda c: _const_spec((1, c))
    return pl.pallas_call(
        _ffn_kernel,
        out_shape=[jax.ShapeDtypeStruct((n, D_MODEL), F32), jax.ShapeDtypeStruct((n, D_MODEL), BF16)],
        grid=(n // tm,),
        in_specs=[row(D_MODEL), row(D_MODEL), _const_spec((D_MODEL, 2 * D_FF)),
                  _const_spec((D_FF, D_MODEL)), vec(D_MODEL), vec(D_MODEL)],
        out_specs=[row(D_MODEL), row(D_MODEL)],
        compiler_params=_cparams("parallel"),
        name="ffn_ln2",
    )(x, x_bf, wp['w_ffn_in'][l], wp['w_ffn_out'][l], wp['ln2_g'][l], wp['ln2_b'][l])


def _block_diag(blocks):
    n, bi, bj = blocks.shape
    eye = jnp.eye(n, dtype=blocks.dtype)
    return (eye[:, None, :, None] * blocks[:, :, None, :]).reshape(n * bi, n * bj)


def _prepare_weights(P):
    rowvec = lambda a: a.reshape(a.shape[0], 1, -1).astype(F32)
    w_in = P['w_in']
    c0, c1, c2, c3 = RW_COLS, RW_COLS + D_LRU, RW_COLS + 2 * D_LRU, RW_COLS + 2 * D_LRU + D_XA
    zeros_wa = jnp.zeros((DEPTH, W_LORA, D_RW), F32)
    head_red = (jnp.arange(D_RW)[:, None] // RW_HEAD == jnp.arange(LANES)[None, :]).astype(F32)
    wp = {
        'w_rw': w_in[:, :, :c0].astype(BF16),
        'w_lru': w_in[:, :, c0:c2].astype(BF16),
        'w_q': w_in[:, :, c2:c3].astype(BF16),
        'w_gate': w_in[:, :, c3:].astype(BF16),
        'w_lora': jnp.concatenate(
            [jnp.concatenate([P['rw_w2'], zeros_wa], axis=2),
             jnp.concatenate([zeros_wa, P['rw_a2']], axis=2)], axis=1).astype(BF16),
        'rw_g2': P['rw_g2'].astype(BF16),
        'rw_v1': jnp.pad(P['rw_v1'], ((0, 0), (0, 0), (0, LANES - V_LORA))).astype(BF16),
        'rw_v2': jnp.pad(P['rw_v2'], ((0, 0), (0, LANES - V_LORA), (0, 0))).astype(BF16),
        'rw_rk': rowvec(P['rw_rk'].reshape(DEPTH, D_RW)),
        'head_red': head_red.astype(BF16),
        'head_exp': head_red.T.astype(BF16),
        'w_lru_gate': jnp.concatenate(
            [jax.vmap(_block_diag)(P['lru_w_rg']), jax.vmap(_block_diag)(P['lru_w_ig'])],
            axis=2).astype(BF16),
        'lru_conv_w': P['lru_conv_w'].astype(F32),
        'w_mem_kv': P['w_mem_kv'].astype(BF16),
    }
    for name in ('w_rw_out', 'w_lru_out', 'w_xa_out', 'w_o', 'w_ffn_in', 'w_ffn_out'):
        wp[name] = P[name].astype(BF16)
    for name in ('rw_mu', 'rw_w0', 'rw_a0', 'rw_v0', 'rw_kk', 'rw_ka', 'rw_gn_g', 'rw_gn_b',
                 'lru_conv_b', 'lru_b_rg', 'lru_b_ig', 'lru_lambda', 'ln1_g', 'ln1_b', 'ln2_g', 'ln2_b'):
        wp[name] = rowvec(P[name])
    return wp


def _to_scan_layout(a, batch, t_len, time_major):
    if time_major:
        a = a.reshape(t_len, batch, RW_HEADS, RW_HEAD).transpose(0, 3, 1, 2)
    else:
        a = a.reshape(batch, t_len, RW_HEADS, RW_HEAD).transpose(1, 3, 0, 2)
    a = a.reshape(t_len, RW_HEAD, batch * RW_HEADS)
    pad = -(batch * RW_HEADS) % LANES
    return jnp.pad(a, ((0, 0), (0, 0), (0, pad)))


def _from_scan_layout(y, batch, t_len, time_major):
    y = y[:, :, :batch * RW_HEADS].reshape(t_len, RW_HEAD, batch, RW_HEADS)
    y = y.transpose(0, 2, 3, 1) if time_major else y.transpose(2, 0, 3, 1)
    return y.reshape(batch * t_len, D_RW)


def _state_to_scan_layout(s, batch):
    s = s.transpose(3, 2, 0, 1).reshape(RW_HEAD, RW_HEAD, batch * RW_HEADS)
    pad = -(batch * RW_HEADS) % LANES
    return jnp.pad(s, ((0, 0), (0, 0), (0, pad)))


def _state_from_scan_layout(s, batch):
    s = s[:, :, :batch * RW_HEADS].reshape(RW_HEAD, RW_HEAD, batch, RW_HEADS)
    return s.transpose(2, 3, 1, 0)


def _trunk(x3, mem_k, mem_v, shift, wkv, conv, h, wp, time_major):
    batch, t_len, _ = x3.shape
    assert t_len >= CONV_W - 1
    n = batch * t_len
    hist_steps = CONV_W - 1
    if time_major:
        groups, G, tiles, tm = 1, batch, 1, n
        x = x3.transpose(1, 0, 2).reshape(n, D_MODEL)
        tm_rows = min(n, ROW_TILE)
    else:
        tm = min(t_len, ROW_TILE)
        groups, G, tiles = batch, 1, t_len // tm
        x = x3.reshape(n, D_MODEL)
        tm_rows = tm
    geom = (groups, G, tiles, tm)
    x_bf = x.astype(BF16)
    v_first = None
    n_sh, n_wkv, n_conv, n_h = [], [], [], []
    for l in range(DEPTH):
        if time_major:
            prev = shift[l][None]
            conv0 = conv[l].transpose(1, 0, 2).reshape(1, hist_steps * batch, D_LRU)
            h0 = h[l][None]
        else:
            prev = jnp.pad(shift[l][:, None, :], ((0, 0), (SUBLANES - 1, 0), (0, 0)))
            conv0 = jnp.pad(conv[l], ((0, 0), (SUBLANES - hist_steps, 0), (0, 0)))
            h0 = h[l][:, None, :]
        r, w, k, v, nkk, b, g, bonus, sh_tail = _rw_prep(x_bf, prev, wp, l, geom, v_first,
                                                         log_decay=not time_major)
        if l == 0:
            v_first = v
        if time_major:
            ops = [_to_scan_layout(a, batch, t_len, True) for a in (r, w, k, v, nkk, b)]
            y_s, s_new = _wkv_scan(ops, _state_to_scan_layout(wkv[l], batch))
            y = _from_scan_layout(y_s, batch, t_len, True)
            n_wkv.append(_state_from_scan_layout(s_new, batch))
        else:
            y, s_new = _wkv_chunked((r, w, k, v, nkk, b), _state_to_pairs(wkv[l]), batch, t_len)
            n_wkv.append(_state_from_pairs(s_new))
        n_sh.append(sh_tail[0] if time_major else sh_tail[:, -1, :])
        hg, conv_tail, h_last = _lru(x_bf, conv0, h0, wp, l, geom)
        if time_major:
            n_conv.append(conv_tail[0].reshape(hist_steps, batch, D_LRU).transpose(1, 0, 2))
            n_h.append(h_last[0])
        else:
            n_conv.append(conv_tail[:, SUBLANES - hist_steps:, :])
            n_h.append(h_last[:, 0, :])
        q = _matmul(x_bf, wp['w_q'][l], BF16, "q_proj")
        if time_major:
            q3 = q.reshape(t_len, batch, D_XA).transpose(1, 0, 2)
            t_pad = -t_len % (2 * SUBLANES)
            q3 = jnp.pad(q3, ((0, 0), (0, t_pad), (0, 0)))
            o3 = _attention_cache(q3, mem_k, mem_v, l, ATTN_CACHE_BATCH)
            oatt = o3[:, :t_len].transpose(1, 0, 2).reshape(n, D_XA)
        else:
            q3 = q.reshape(batch, t_len, D_XA)
            oatt = _attention_packed(q3, mem_k[l], min(t_len, 1024)).reshape(n, D_XA)
        x, x_bf = _merge(x, x_bf, y, bonus, g, hg, oatt, wp, l, min(n, MERGE_ROW_TILE))
        x, x_bf = _ffn(x, x_bf, wp, l, min(n, FFN_ROW_TILE))
    if time_major:
        y3 = x.reshape(t_len, batch, D_MODEL).transpose(1, 0, 2)
    else:
        y3 = x.reshape(batch, t_len, D_MODEL)
    return y3, jnp.stack(n_sh), jnp.stack(n_wkv), jnp.stack(n_conv), jnp.stack(n_h)


def kernel(x_prompt, x_sample, mem_prompt, state_rwkv_shift, state_rwkv_wkv, state_lru_conv,
           state_lru_h, cache_mem_k, cache_mem_v, w_in, rw_mu, rw_w0, rw_w2, rw_a0, rw_a2,
           rw_g2, rw_v0, rw_v1, rw_v2, rw_kk, rw_ka, rw_rk, rw_gn_g, rw_gn_b, w_rw_out,
           lru_conv_w, lru_conv_b, lru_w_rg, lru_b_rg, lru_w_ig, lru_b_ig, lru_lambda, w_lru_out,
           w_mem_kv, w_xa_out, w_o, ln1_g, ln1_b, w_ffn_in, w_ffn_out, ln2_g, ln2_b):
    P = dict(w_in=w_in, rw_mu=rw_mu, rw_w0=rw_w0, rw_w2=rw_w2, rw_a0=rw_a0, rw_a2=rw_a2,
             rw_g2=rw_g2, rw_v0=rw_v0, rw_v1=rw_v1, rw_v2=rw_v2, rw_kk=rw_kk, rw_ka=rw_ka,
             rw_rk=rw_rk, rw_gn_g=rw_gn_g, rw_gn_b=rw_gn_b, w_rw_out=w_rw_out,
             lru_conv_w=lru_conv_w, lru_conv_b=lru_conv_b, lru_w_rg=lru_w_rg, lru_b_rg=lru_b_rg,
             lru_w_ig=lru_w_ig, lru_b_ig=lru_b_ig, lru_lambda=lru_lambda, w_lru_out=w_lru_out,
             w_mem_kv=w_mem_kv, w_xa_out=w_xa_out, w_o=w_o, ln1_g=ln1_g, ln1_b=ln1_b,
             w_ffn_in=w_ffn_in, w_ffn_out=w_ffn_out, ln2_g=ln2_g, ln2_b=ln2_b)
    wp = _prepare_weights(P)

    bp, n_mem, _ = mem_prompt.shape
    mem_bf = mem_prompt.reshape(bp * n_mem, D_MODEL).astype(BF16)
    kv = jnp.stack([_matmul(mem_bf, wp['w_mem_kv'][l], F32, "mem_kv").reshape(bp, n_mem, 2 * D_XA)
                    for l in range(DEPTH)])
    new_mem_k = kv[..., :D_XA].reshape(DEPTH, bp, n_mem, XA_HEADS, XA_HEAD)
    new_mem_v = kv[..., D_XA:].reshape(DEPTH, bp, n_mem, XA_HEADS, XA_HEAD)
    dt = x_prompt.dtype
    z_shift = jnp.zeros((DEPTH, bp, RW_COLS), dt)
    z_wkv = jnp.zeros((DEPTH, bp, RW_HEADS, RW_HEAD, RW_HEAD), dt)
    z_conv = jnp.zeros((DEPTH, bp, CONV_W - 1, D_LRU), dt)
    z_h = jnp.zeros((DEPTH, bp, D_LRU), dt)
    y_prompt, sh_p, wkv_p, conv_p, h_p = _trunk(
        x_prompt, kv, None, z_shift, z_wkv, z_conv, z_h, wp, time_major=False)

    y_sample, sh_s, wkv_s, conv_s, h_s = _trunk(
        x_sample, cache_mem_k, cache_mem_v, state_rwkv_shift, state_rwkv_wkv, state_lru_conv,
        state_lru_h, wp, time_major=True)
    return (y_prompt, y_sample, sh_p, wkv_p, conv_p, h_p, new_mem_k, new_mem_v,
            sh_s, wkv_s, conv_s, h_s)
```

```python
import functools

import jax
import jax.numpy as jnp
from jax import lax
from jax.experimental import pallas as pl
from jax.experimental.pallas import tpu as pltpu

F32 = jnp.float32
BF16 = jnp.bfloat16

D_MODEL = 1024
DEPTH = 4
N_MEM = 256
RW_HEADS = 12
RW_HEAD = 64
D_RW = RW_HEADS * RW_HEAD
W_LORA = 64
A_LORA = 64
V_LORA = 32
G_LORA = 128
RW_COLS = 3 * D_RW + W_LORA + A_LORA + G_LORA
GN_EPS = 64e-5
LRU_BLOCKS = 12
LRU_BW = 64
D_LRU = LRU_BLOCKS * LRU_BW
CONV_W = 4
LRU_C = 8.0
XA_HEADS = 4
XA_HEAD = 128
D_XA = XA_HEADS * XA_HEAD
N_BRANCH = 3
D_FF = 2816
ALPHA = (2 * DEPTH) ** 0.25
LN_EPS = 1e-5

LANES = 128
SUBLANES = 8
VMEM_LIMIT_BYTES = 52 * 1024 * 1024

ROW_TILE = 256
MERGE_ROW_TILE = 256
FFN_ROW_TILE = 512
FFN_CHUNK = 1408
SCAN_T_BLOCK = 32
SCAN_K_UNROLL = 4
WKV_CHUNK_LOG2 = 6
WKV_CHUNK = 1 << WKV_CHUNK_LOG2
WKV_ROW_TILE = 128
WKV_SEQS_PER_STEP = 4
ATTN_CACHE_BATCH = 8
ATTN_MASKED = -1e30


def _cparams(*sem):
    return pltpu.CompilerParams(dimension_semantics=sem, vmem_limit_bytes=VMEM_LIMIT_BYTES)


def _const_spec(shape):
    nd = len(shape)
    return pl.BlockSpec(shape, lambda *_: (0,) * nd, pipeline_mode=pl.Buffered(1))


def _dot(a, b):
    return jnp.dot(a, b, preferred_element_type=F32)


def _split_bf16(x):
    hi = x.astype(BF16)
    return hi, (x - hi.astype(F32)).astype(BF16)


def _head_sum(x, red_ref, exp_ref):
    hi, lo = _split_bf16(x)
    red = red_ref[...]
    s_hi, s_lo = _split_bf16(_dot(hi, red) + _dot(lo, red))
    ex = exp_ref[...]
    return _dot(s_hi, ex) + _dot(s_lo, ex)


def _layer_norm_rows(x, g, b, eps):
    mu = jnp.mean(x, axis=-1, keepdims=True)
    d = x - mu
    var = jnp.mean(d * d, axis=-1, keepdims=True)
    return d * lax.rsqrt(var + eps) * g + b


def _delayed_rows(p, hist, j, G):
    n = p.shape[0]
    h = hist.shape[0] // G
    if G % SUBLANES == 0:
        ext = jnp.concatenate([hist, p], axis=0)
        return ext[(h - j) * G:(h - j) * G + n]
    assert G == 1
    out = pltpu.roll(p, j, 0)
    row = lax.broadcasted_iota(jnp.int32, (n, 1), 0)
    for i in range(j):
        out = jnp.where(row == i, hist[h - j + i:h - j + i + 1, :], out)
    return out


def _tail_rows(G, steps):
    return SUBLANES if G == 1 else steps * G


def _mm_kernel(x_ref, w_ref, o_ref):
    o_ref[...] = _dot(x_ref[...], w_ref[...]).astype(o_ref.dtype)


def _matmul(x_bf, w_bf, out_dtype, name):
    n, k = x_bf.shape
    m = w_bf.shape[1]
    tm = min(n, 512)
    return pl.pallas_call(
        _mm_kernel,
        out_shape=jax.ShapeDtypeStruct((n, m), out_dtype),
        grid=(n // tm,),
        in_specs=[pl.BlockSpec((tm, k), lambda i: (i, 0)), _const_spec((k, m))],
        out_specs=pl.BlockSpec((tm, m), lambda i: (i, 0)),
        compiler_params=_cparams("parallel"),
        name=name,
    )(x_bf, w_bf)


def _rw_prep_kernel(first_layer, G, log_decay, *refs):
    scan_layout = G != 1
    n_in = 14 if first_layer else 18
    (x_ref, wrw_ref, wq_ref, prev_ref, mu_ref, wlora_ref, w0_ref, a0_ref, g2_ref,
     kkp_ref, kap_ref, rk_ref, red_ref, exp_ref) = refs[:14]
    if not first_layer:
        vfirst_ref, v0_ref, v1_ref, v2_ref = refs[14:18]
    (r_out, w_out, k_out, v_out, nkk_out, b_out, g_out, bonus_out, tail_out, q_out) = refs[n_in:n_in + 10]
    carry_ref = refs[-1]

    @pl.when(pl.program_id(1) == 0)
    def _():
        carry_ref[...] = prev_ref[0]

    x_bf = x_ref[...]
    q_out[...] = _dot(x_bf, wq_ref[...]).astype(q_out.dtype)
    p = _dot(x_bf, wrw_ref[...])
    n = p.shape[0]
    p_prev = _delayed_rows(p, carry_ref[...], 1, G)
    tail = _tail_rows(G, 1)
    carry_ref[...] = p[n - tail:, :]
    tail_out[0] = p[n - tail:, :]
    xs = p + (p_prev - p) * mu_ref[...]

    r = xs[:, 0:D_RW]
    k = xs[:, D_RW:2 * D_RW]
    v = xs[:, 2 * D_RW:3 * D_RW]
    z = xs[:, 3 * D_RW:3 * D_RW + W_LORA + A_LORA]
    gd = xs[:, 3 * D_RW + W_LORA + A_LORA:]

    lane = lax.broadcasted_iota(jnp.int32, (1, W_LORA + A_LORA), 1)
    zt = jnp.where(lane < W_LORA, jnp.tanh(z), z)
    lo = _dot(zt.astype(BF16), wlora_ref[...])
    w = -jax.nn.softplus(-(w0_ref[...] + lo[:, :D_RW])) - 0.5
    decay = -jnp.exp(w) if log_decay else jnp.exp(-jnp.exp(w))
    a = jax.nn.sigmoid(a0_ref[...] + lo[:, D_RW:])
    g = _dot(jax.nn.sigmoid(gd).astype(BF16), g2_ref[...])

    if not first_layer:
        vv = _dot(_dot(v.astype(BF16), v1_ref[...]).astype(BF16), v2_ref[...])
        v = v + (vfirst_ref[...] - v) * jax.nn.sigmoid(v0_ref[...] + vv)

    kk = k * kkp_ref[...]
    kk = kk / jnp.maximum(jnp.sqrt(_head_sum(kk * kk, red_ref, exp_ref)), 1e-12)
    k2 = k * (1.0 + (a - 1.0) * kap_ref[...])
    bonus = _head_sum(r * k2 * rk_ref[...], red_ref, exp_ref) * v

    scan_ops = ((r_out, r), (w_out, decay), (k_out, k2), (v_out, v), (nkk_out, -kk), (b_out, kk * a))
    if scan_layout:
        for s in range(n // G):
            for ref, val in scan_ops:
                ref[s] = val[s * G:(s + 1) * G, :].T
        refs[n_in + 10][...] = v
    else:
        for ref, val in scan_ops:
            ref[...] = val
    g_out[...] = g
    bonus_out[...] = bonus


def _rw_prep(x_bf, prev, wp, l, geom, v_first, log_decay):
    groups, G, tiles, tm = geom
    n = x_bf.shape[0]
    first = v_first is None
    scan_layout = G != 1
    row = lambda c: pl.BlockSpec((tm, c), lambda b, j: (b * tiles + j, 0))
    vec = lambda c: _const_spec((1, c))
    tail = _tail_rows(G, 1)
    in_specs = [row(D_MODEL), _const_spec((D_MODEL, RW_COLS)), _const_spec((D_MODEL, D_XA)),
                pl.BlockSpec((1, tail, RW_COLS), lambda b, j: (b, 0, 0)),
                vec(RW_COLS), _const_spec((W_LORA + A_LORA, 2 * D_RW)), vec(D_RW), vec(D_RW),
                _const_spec((G_LORA, D_RW)), vec(D_RW), vec(D_RW), vec(D_RW),
                _const_spec((D_RW, LANES)), _const_spec((LANES, D_RW))]
    args = [x_bf, wp['w_rw'][l], wp['w_q'][l], prev, wp['rw_mu'][l], wp['w_lora'][l], wp['rw_w0'][l],
            wp['rw_a0'][l], wp['rw_g2'][l], wp['rw_kk'][l], wp['rw_ka'][l], wp['rw_rk'][l],
            wp['head_red'], wp['head_exp']]
    if not first:
        in_specs += [row(D_RW), vec(D_RW), _const_spec((D_RW, LANES)), _const_spec((LANES, D_RW))]
        args += [v_first, wp['rw_v0'][l - 1], wp['rw_v1'][l - 1], wp['rw_v2'][l - 1]]
    if scan_layout:
        steps = n // G
        scan_shape = [jax.ShapeDtypeStruct((steps, D_RW, G), F32)] * 6
        scan_specs = [pl.BlockSpec((steps, D_RW, G), lambda b, j: (0, 0, 0))] * 6
    else:
        scan_shape = [jax.ShapeDtypeStruct((n, D_RW), F32)] * 6
        scan_specs = [row(D_RW)] * 6
    out_shape = scan_shape + [jax.ShapeDtypeStruct((n, D_RW), F32)] * 2 + [
        jax.ShapeDtypeStruct((groups, tail, RW_COLS), F32), jax.ShapeDtypeStruct((n, D_XA), BF16)]
    out_specs = scan_specs + [row(D_RW)] * 2 + [
        pl.BlockSpec((1, tail, RW_COLS), lambda b, j: (b, 0, 0)), row(D_XA)]
    if scan_layout:
        out_shape.append(jax.ShapeDtypeStruct((n, D_RW), F32))
        out_specs.append(row(D_RW))
    return pl.pallas_call(
        functools.partial(_rw_prep_kernel, first, G, log_decay),
        out_shape=out_shape,
        grid=(groups, tiles),
        in_specs=in_specs,
        out_specs=out_specs,
        scratch_shapes=[pltpu.VMEM((tail, RW_COLS), F32)],
        compiler_params=_cparams("parallel", "arbitrary"),
        name="rw_prep",
    )(*args)


def _wkv_scan_kernel(r_ref, w_ref, k_ref, v_ref, nkk_ref, b_ref, s0_ref, y_ref, s_out_ref):
    tt, _, lanes = r_ref.shape
    vt = RW_HEAD // SUBLANES

    @pl.when(pl.program_id(1) == 0)
    def _():
        s_out_ref[...] = s0_ref[...]

    def bcast(ref, t, kidx):
        return jnp.broadcast_to(ref[t, pl.ds(kidx, 1), :], (SUBLANES, lanes))

    def step(t, carry):
        def sa_body(kidx, sa):
            nk = bcast(nkk_ref, t, kidx)
            return tuple(sa[i] + s_out_ref[kidx, i * SUBLANES:(i + 1) * SUBLANES, :] * nk
                         for i in range(vt))
        zeros = tuple(jnp.zeros((SUBLANES, lanes), F32) for _ in range(vt))
        sa = lax.fori_loop(0, RW_HEAD, sa_body, zeros, unroll=SCAN_K_UNROLL)
        vv = tuple(v_ref[t, i * SUBLANES:(i + 1) * SUBLANES, :] for i in range(vt))

        def up_body(kidx, y):
            wk = bcast(w_ref, t, kidx)
            bk = bcast(b_ref, t, kidx)
            kk = bcast(k_ref, t, kidx)
            rk = bcast(r_ref, t, kidx)
            out = []
            for i in range(vt):
                sl = slice(i * SUBLANES, (i + 1) * SUBLANES)
                s_new = s_out_ref[kidx, sl, :] * wk + sa[i] * bk + vv[i] * kk
                s_out_ref[kidx, sl, :] = s_new
                out.append(y[i] + s_new * rk)
            return tuple(out)
        y = lax.fori_loop(0, RW_HEAD, up_body, zeros, unroll=SCAN_K_UNROLL)
        for i in range(vt):
            y_ref[t, i * SUBLANES:(i + 1) * SUBLANES, :] = y[i]
        return carry

    lax.fori_loop(0, tt, step, 0)


def _wkv_scan(ops, s0):
    t_len, _, batch = ops[0].shape
    tt = min(t_len, SCAN_T_BLOCK)
    op_spec = pl.BlockSpec((tt, RW_HEAD, batch), lambda h, j: (j, h, 0))
    st_spec = pl.BlockSpec((None, RW_HEAD, RW_HEAD, batch), lambda h, j: (h, 0, 0, 0))
    return pl.pallas_call(
        _wkv_scan_kernel,
        out_shape=[jax.ShapeDtypeStruct((t_len, D_RW, batch), F32),
                   jax.ShapeDtypeStruct((RW_HEADS, RW_HEAD, RW_HEAD, batch), F32)],
        grid=(RW_HEADS, t_len // tt),
        in_specs=[op_spec] * 6 + [st_spec],
        out_specs=[op_spec, st_spec],
        compiler_params=_cparams("parallel", "arbitrary"),
        name="wkv_scan",
    )(*ops, s0)


def _wkv_chunk_kernel(r_ref, lw_ref, k_ref, v_ref, a_ref, b_ref, s0_ref, y_ref, s_out_ref, s_scr):
    nb, tm, _ = r_ref.shape
    C = WKV_CHUNK
    n_pairs = D_RW // LANES

    @pl.when(pl.program_id(1) == 0)
    def _():
        s_scr[...] = s0_ref[...]

    ri = lax.broadcasted_iota(jnp.int32, (C, C), 0)
    ci = lax.broadcasted_iota(jnp.int32, (C, C), 1)
    tri = (ci <= ri).astype(BF16)
    r2 = lax.broadcasted_iota(jnp.int32, (2 * C, 2 * C), 0) % C
    c2 = lax.broadcasted_iota(jnp.int32, (2 * C, 2 * C), 1) % C
    strict = c2 < r2
    incl = c2 <= r2
    r4 = lax.broadcasted_iota(jnp.int32, (4 * C, 2 * C), 0)
    c4 = lax.broadcasted_iota(jnp.int32, (4 * C, 2 * C), 1) % C
    strict_incl = (c4 < r4 % C) | ((r4 >= 2 * C) & (c4 == r4 % C))
    eye = (lax.broadcasted_iota(jnp.int32, (2 * C, 2 * C), 0)
           == lax.broadcasted_iota(jnp.int32, (2 * C, 2 * C), 1)).astype(F32)
    head0 = lax.broadcasted_iota(jnp.int32, (1, LANES), 1) < RW_HEAD

    def stack(z):
        return jnp.concatenate([jnp.where(head0, z, 0.0), jnp.where(head0, 0.0, z)], axis=0)

    def dot_t(x, y):
        return lax.dot_general(x, y, (((1,), (1,)), ((), ())), preferred_element_type=F32)

    def chunk_body(c, carry):
        rows = pl.ds(pl.multiple_of(c * C, C), C)
        units = [(i, p) for i in range(nb) for p in range(n_pairs)]
        pairs = range(len(units))
        a_st, r_st, b_st, k_st, v_st, pc = [], [], [], [], [], []
        for i in range(nb):
            lw_all = lw_ref[i, rows, :]
            hi, lo = _split_bf16(lw_all)
            cum_all = _dot(tri, hi) + _dot(tri, lo)
            for p in range(n_pairs):
                lanes = slice(p * LANES, (p + 1) * LANES)
                cum = cum_all[:, lanes]
                e_pos = jnp.exp(cum)
                e_neg = jnp.exp(-cum)
                e_prev = jnp.exp(cum - lw_all[:, lanes])
                pc.append(e_pos[C - 1:C, :])
                a_st.append(stack(a_ref[i, rows, lanes] * e_prev).astype(BF16))
                r_st.append(stack(r_ref[i, rows, lanes] * e_pos).astype(BF16))
                b_st.append(stack(b_ref[i, rows, lanes] * e_neg))
                k_st.append(stack(k_ref[i, rows, lanes] * e_neg))
                v_st.append(stack(v_ref[i, rows, lanes]).astype(BF16))

        ar = [jnp.concatenate([a_st[p], r_st[p]], axis=0) for p in pairs]
        g = [dot_t(ar[p], jnp.concatenate([b_st[p], k_st[p]], axis=0).astype(BF16))
             for p in pairs]
        n_mat = [jnp.where(strict, g[p][:2 * C, :2 * C], 0.0) for p in pairs]
        m_rb = [jnp.where(incl, g[p][2 * C:, :2 * C], 0.0).astype(BF16) for p in pairs]
        m_akrk = [jnp.where(strict_incl, g[p][:, 2 * C:], 0.0).astype(BF16) for p in pairs]

        t_inv = [eye + n_mat[p] for p in pairs]
        pw = [n_mat[p].astype(BF16) for p in pairs]
        pw = [_dot(pw[p], pw[p]).astype(BF16) for p in pairs]
        for _ in range(WKV_CHUNK_LOG2 - 2):
            both = [_dot(jnp.concatenate([pw[p], t_inv[p].astype(BF16)], axis=0), pw[p]) for p in pairs]
            pw = [both[p][:2 * C].astype(BF16) for p in pairs]
            t_inv = [t_inv[p] + both[p][2 * C:] for p in pairs]
        t_inv = [(t_inv[p] + _dot(t_inv[p].astype(BF16), pw[p])).astype(BF16) for p in pairs]

        s = [s_scr[i, p] for i, p in units]
        xy = [dot_t(ar[p], s[p].astype(BF16)) + _dot(m_akrk[p], v_st[p]) for p in pairs]
        u = [_dot(t_inv[p], xy[p][:2 * C].astype(BF16)).astype(BF16) for p in pairs]
        for p in pairs:
            i, hp = units[p]
            y_st = xy[p][2 * C:] + _dot(m_rb[p], u[p])
            y_ref[i, rows, hp * LANES:(hp + 1) * LANES] = y_st[:C] + y_st[C:]
        for p in pairs:
            i, hp = units[p]
            uv = jnp.concatenate([u[p], v_st[p]], axis=0)
            bk = jnp.concatenate([b_st[p] * pc[p], k_st[p] * pc[p]], axis=0).astype(BF16)
            s_scr[i, hp] = s[p] * pc[p] + lax.dot_general(uv, bk, (((0,), (0,)), ((), ())),
                                                          preferred_element_type=F32)
        return carry

    lax.fori_loop(0, tm // C, chunk_body, 0)
    s_out_ref[...] = s_scr[...]


def _wkv_chunked(ops, s0_pairs, batch, t_len):
    n_pairs = D_RW // LANES
    nb = WKV_SEQS_PER_STEP if batch % WKV_SEQS_PER_STEP == 0 else 1
    tm = min(t_len, WKV_ROW_TILE)
    ops = [a.reshape(batch, t_len, D_RW) for a in ops]
    row = pl.BlockSpec((nb, tm, D_RW), lambda b, j: (b, j, 0))
    st = pl.BlockSpec((nb, n_pairs, LANES, LANES), lambda b, j: (b, 0, 0, 0))
    y, s_new = pl.pallas_call(
        _wkv_chunk_kernel,
        out_shape=[jax.ShapeDtypeStruct((batch, t_len, D_RW), F32),
                   jax.ShapeDtypeStruct((batch, n_pairs, LANES, LANES), F32)],
        grid=(batch // nb, t_len // tm),
        in_specs=[row] * 6 + [st],
        out_specs=[row, st],
        scratch_shapes=[pltpu.VMEM((nb, n_pairs, LANES, LANES), F32)],
        compiler_params=_cparams("parallel", "arbitrary"),
        name="wkv_chunked",
    )(*ops, s0_pairs)
    return y.reshape(batch * t_len, D_RW), s_new


def _state_to_pairs(s):
    batch = s.shape[0]
    s = s.reshape(batch, RW_HEADS // 2, 2, RW_HEAD, RW_HEAD)
    eye = jnp.eye(2, dtype=s.dtype)
    out = s[:, :, :, :, None, :] * eye[None, None, :, None, :, None]
    return out.reshape(batch, RW_HEADS // 2, LANES, LANES)


def _state_from_pairs(sp):
    batch = sp.shape[0]
    sp = sp.reshape(batch, RW_HEADS // 2, 2, RW_HEAD, 2, RW_HEAD)
    out = jnp.stack([sp[:, :, 0, :, 0, :], sp[:, :, 1, :, 1, :]], axis=2)
    return out.reshape(batch, RW_HEADS, RW_HEAD, RW_HEAD)


def _lru_kernel(G, x_ref, wl_ref, conv0_ref, h0_ref, cw_ref, cb_ref, wgate_ref, brg_ref, big_ref,
                lam_ref, hg_out, conv_out, h_out, hist_ref, hcar_ref, a_scr, u_scr, h_scr):
    hist_steps = CONV_W - 1

    @pl.when(pl.program_id(1) == 0)
    def _():
        hist_ref[...] = conv0_ref[0]
        hcar_ref[...] = h0_ref[0]

    p = _dot(x_ref[...], wl_ref[...])
    n = p.shape[0]
    p_lx = p[:, :D_LRU]
    p_lg = p[:, D_LRU:]
    hist = hist_ref[...]
    cw = cw_ref[...]
    xc = cb_ref[...] + cw[CONV_W - 1:CONV_W, :] * p_lx
    for j in range(hist_steps):
        xc = xc + cw[j:j + 1, :] * _delayed_rows(p_lx, hist, hist_steps - j, G)
    tail = _tail_rows(G, hist_steps)
    hist_ref[...] = p_lx[n - tail:, :]
    conv_out[0] = p_lx[n - tail:, :]

    gates = _dot(xc.astype(BF16), wgate_ref[...])
    rg = jax.nn.sigmoid(gates[:, :D_LRU] + brg_ref[...])
    ig = jax.nn.sigmoid(gates[:, D_LRU:] + big_ref[...])
    log_a = -LRU_C * rg * jax.nn.softplus(-lam_ref[...])
    a = jnp.exp(log_a)
    a_scr[...] = a
    u_scr[...] = jnp.sqrt(-jnp.tanh(log_a) * (a * a + 1.0)) * (ig * xc)

    steps = n // G

    if G == 1:
        sub = lax.broadcasted_iota(jnp.int32, (SUBLANES, 1), 0)

        def step(s, h):
            rows = pl.ds(pl.multiple_of(s * SUBLANES, SUBLANES), SUBLANES)
            a_t = a_scr[rows, :]
            u_t = u_scr[rows, :]
            for d in (1, 2, 4):
                a_prev = jnp.where(sub >= d, pltpu.roll(a_t, d, 0), 1.0)
                u_prev = jnp.where(sub >= d, pltpu.roll(u_t, d, 0), 0.0)
                u_t = a_t * u_prev + u_t
                a_t = a_t * a_prev
            h_rows = a_t * h + u_t
            h_scr[rows, :] = h_rows
            return h_rows[SUBLANES - 1:, :]
        h = lax.fori_loop(0, steps // SUBLANES, step, hcar_ref[...], unroll=4)
    else:
        def step(s, h):
            rows = pl.ds(pl.multiple_of(s * G, G), G)
            h = a_scr[rows, :] * h + u_scr[rows, :]
            h_scr[rows, :] = h
            return h
        h = lax.fori_loop(0, steps, step, hcar_ref[...], unroll=True)
    hcar_ref[...] = h
    h_out[0] = h
    hg_out[...] = (h_scr[...] * jax.nn.gelu(p_lg)).astype(hg_out.dtype)


def _lru(x_bf, conv0, h0, wp, l, geom):
    groups, G, tiles, tm = geom
    n = x_bf.shape[0]
    hist_steps = CONV_W - 1
    tail = _tail_rows(G, hist_steps)
    row = lambda c: pl.BlockSpec((tm, c), lambda b, j: (b * tiles + j, 0))
    vec = lambda c: _const_spec((1, c))
    grp = lambda r, c: pl.BlockSpec((1, r, c), lambda b, j: (b, 0, 0))
    return pl.pallas_call(
        functools.partial(_lru_kernel, G),
        out_shape=[jax.ShapeDtypeStruct((n, D_LRU), BF16),
                   jax.ShapeDtypeStruct((groups, tail, D_LRU), F32),
                   jax.ShapeDtypeStruct((groups, G, D_LRU), F32)],
        grid=(groups, tiles),
        in_specs=[row(D_MODEL), _const_spec((D_MODEL, 2 * D_LRU)),
                  grp(tail, D_LRU), grp(G, D_LRU),
                  _const_spec((CONV_W, D_LRU)), vec(D_LRU), _const_spec((D_LRU, 2 * D_LRU)),
                  vec(D_LRU), vec(D_LRU), vec(D_LRU)],
        out_specs=[row(D_LRU), grp(tail, D_LRU), grp(G, D_LRU)],
        scratch_shapes=[pltpu.VMEM((tail, D_LRU), F32), pltpu.VMEM((G, D_LRU), F32),
                        pltpu.VMEM((tm, D_LRU), F32), pltpu.VMEM((tm, D_LRU), F32),
                        pltpu.VMEM((tm, D_LRU), F32)],
        compiler_params=_cparams("parallel", "arbitrary"),
        name="rg_lru",
    )(x_bf, wp['w_lru'][l], conv0, h0, wp['lru_conv_w'][l], wp['lru_conv_b'][l], wp['w_lru_gate'][l],
      wp['lru_b_rg'][l], wp['lru_b_ig'][l], wp['lru_lambda'][l])


def _attn_heads(q_ref, o_ref, key_of, val_of):
    scale = XA_HEAD ** -0.5
    for h in range(XA_HEADS):
        sl = slice(h * XA_HEAD, (h + 1) * XA_HEAD)
        q = q_ref[:, :, sl]
        k = key_of(h).astype(BF16)
        v = val_of(h).astype(BF16)
        s = jnp.einsum('bqd,bkd->bqk', q, k, preferred_element_type=F32) * scale
        s = s - jnp.max(s, axis=-1, keepdims=True)
        e = jnp.exp(s)
        p = (e / jnp.sum(e, axis=-1, keepdims=True)).astype(BF16)
        o = jnp.einsum('bqk,bkd->bqd', p, v, preferred_element_type=F32)
        o_ref[:, :, sl] = o.astype(o_ref.dtype)


def _attn_packed_kernel(q_ref, kv_ref, o_ref):
    _attn_heads(q_ref, o_ref,
                lambda h: kv_ref[:, :, h * XA_HEAD:(h + 1) * XA_HEAD],
                lambda h: kv_ref[:, :, D_XA + h * XA_HEAD:D_XA + (h + 1) * XA_HEAD])


def _attn_cache_kernel(q_ref, k_ref, v_ref, o_ref):
    bb, tq, _ = q_ref.shape
    n_rows = XA_HEADS * tq
    n_cols = N_MEM * XA_HEADS
    own = (lax.broadcasted_iota(jnp.int32, (n_rows, n_cols), 1) % XA_HEADS
           == lax.broadcasted_iota(jnp.int32, (n_rows, n_cols), 0) // tq)
    scale = XA_HEAD ** -0.5
    for i in range(bb):
        q = jnp.concatenate([q_ref[i, :, h * XA_HEAD:(h + 1) * XA_HEAD] for h in range(XA_HEADS)], axis=0)
        k = k_ref[i].astype(BF16)
        v = v_ref[i].astype(BF16)
        s = lax.dot_general(q, k, (((1,), (1,)), ((), ())), preferred_element_type=F32) * scale
        s = jnp.where(own, s, ATTN_MASKED)
        s = s - jnp.max(s, axis=-1, keepdims=True)
        e = jnp.exp(s)
        p = (e / jnp.sum(e, axis=-1, keepdims=True)).astype(BF16)
        o = _dot(p, v)
        for h in range(XA_HEADS):
            o_ref[i, :, h * XA_HEAD:(h + 1) * XA_HEAD] = o[h * tq:(h + 1) * tq].astype(o_ref.dtype)


def _attention_packed(q3, kv3, tq):
    b, t, _ = q3.shape
    qspec = pl.BlockSpec((1, tq, D_XA), lambda i, j: (i, j, 0))
    return pl.pallas_call(
        _attn_packed_kernel,
        out_shape=jax.ShapeDtypeStruct((b, t, D_XA), BF16),
        grid=(b, t // tq),
        in_specs=[qspec, pl.BlockSpec((1, N_MEM, 2 * D_XA), lambda i, j: (i, 0, 0))],
        out_specs=qspec,
        compiler_params=_cparams("parallel", "parallel"),
        name="mem_attn",
    )(q3, kv3)


def _attention_cache(q3, cache_k, cache_v, l, bb):
    b, t, _ = q3.shape
    depth = cache_k.shape[0]
    cache_k = cache_k.reshape(depth, b, N_MEM * XA_HEADS, XA_HEAD)
    cache_v = cache_v.reshape(depth, b, N_MEM * XA_HEADS, XA_HEAD)
    qspec = pl.BlockSpec((bb, t, D_XA), lambda i: (i, 0, 0))
    cspec = pl.BlockSpec((None, bb, N_MEM * XA_HEADS, XA_HEAD), lambda i: (l, i, 0, 0))
    return pl.pallas_call(
        _attn_cache_kernel,
        out_shape=jax.ShapeDtypeStruct((b, t, D_XA), BF16),
        grid=(b // bb,),
        in_specs=[qspec, cspec, cspec],
        out_specs=qspec,
        compiler_params=_cparams("parallel"),
        name="mem_attn_cache",
    )(q3, cache_k, cache_v)


def _merge_kernel(x_ref, xbf_ref, wg_ref, y_ref, bonus_ref, g_ref, gng_ref, gnb_ref, hg_ref,
                  oatt_ref, wrw_ref, wlru_ref, wxa_ref, wo_ref, red_ref, exp_ref, lng_ref, lnb_ref,
                  xo_ref, xobf_ref):
    gates = jax.nn.sigmoid(_dot(xbf_ref[...], wg_ref[...]))
    y = y_ref[...]
    inv = 1.0 / RW_HEAD
    d = y - _head_sum(y, red_ref, exp_ref) * inv
    var = _head_sum(d * d, red_ref, exp_ref) * inv
    yn = d * lax.rsqrt(var + GN_EPS) * gng_ref[...] + gnb_ref[...]
    o_rw = _dot(((yn + bonus_ref[...]) * g_ref[...]).astype(BF16), wrw_ref[...])
    o_lru = _dot(hg_ref[...], wlru_ref[...])
    o_xa = _dot(oatt_ref[...], wxa_ref[...])
    m = (gates[:, :D_MODEL] * o_rw + gates[:, D_MODEL:2 * D_MODEL] * o_lru
         + gates[:, 2 * D_MODEL:] * o_xa)
    mix = _dot(m.astype(BF16), wo_ref[...])
    xo = _layer_norm_rows(ALPHA * x_ref[...] + mix, lng_ref[...], lnb_ref[...], LN_EPS)
    xo_ref[...] = xo
    xobf_ref[...] = xo.astype(BF16)


def _merge(x, x_bf, y, bonus, g, hg, oatt, wp, l, tm):
    n = x.shape[0]
    row = lambda c: pl.BlockSpec((tm, c), lambda i: (i, 0))
    vec = lambda c: _const_spec((1, c))
    return pl.pallas_call(
        _merge_kernel,
        out_shape=[jax.ShapeDtypeStruct((n, D_MODEL), F32), jax.ShapeDtypeStruct((n, D_MODEL), BF16)],
        grid=(n // tm,),
        in_specs=[row(D_MODEL), row(D_MODEL), _const_spec((D_MODEL, N_BRANCH * D_MODEL)),
                  row(D_RW), row(D_RW), row(D_RW), vec(D_RW), vec(D_RW), row(D_LRU), row(D_XA),
                  _const_spec((D_RW, D_MODEL)), _const_spec((D_LRU, D_MODEL)),
                  _const_spec((D_XA, D_MODEL)), _const_spec((D_MODEL, D_MODEL)),
                  _const_spec((D_RW, LANES)), _const_spec((LANES, D_RW)), vec(D_MODEL), vec(D_MODEL)],
        out_specs=[row(D_MODEL), row(D_MODEL)],
        compiler_params=_cparams("parallel"),
        name="merge_ln1",
    )(x, x_bf, wp['w_gate'][l], y, bonus, g, wp['rw_gn_g'][l], wp['rw_gn_b'][l], hg, oatt,
      wp['w_rw_out'][l], wp['w_lru_out'][l], wp['w_xa_out'][l], wp['w_o'][l], wp['head_red'],
      wp['head_exp'],
      wp['ln1_g'][l], wp['ln1_b'][l])


def _ffn_kernel(x_ref, xbf_ref, win_ref, wout_ref, lng_ref, lnb_ref, xo_ref, xobf_ref):
    xb = xbf_ref[...]
    acc = None
    for c in range(D_FF // FFN_CHUNK):
        u = _dot(xb, win_ref[:, c * FFN_CHUNK:(c + 1) * FFN_CHUNK])
        gt = _dot(xb, win_ref[:, D_FF + c * FFN_CHUNK:D_FF + (c + 1) * FFN_CHUNK])
        part = _dot((jax.nn.silu(gt) * u).astype(BF16), wout_ref[c * FFN_CHUNK:(c + 1) * FFN_CHUNK, :])
        acc = part if acc is None else acc + part
    xo = _layer_norm_rows(ALPHA * x_ref[...] + acc, lng_ref[...], lnb_ref[...], LN_EPS)
    xo_ref[...] = xo
    xobf_ref[...] = xo.astype(BF16)


def _ffn(x, x_bf, wp, l, tm):
    n = x.shape[0]
    row = lambda c: pl.BlockSpec((tm, c), lambda i: (i, 0))
    vec = lambda c: _const_spec((1, c))
    return pl.pallas_call(
        _ffn_kernel,
        out_shape=[jax.ShapeDtypeStruct((n, D_MODEL), F32), jax.ShapeDtypeStruct((n, D_MODEL), BF16)],
        grid=(n // tm,),
        in_specs=[row(D_MODEL), row(D_MODEL), _const_spec((D_MODEL, 2 * D_FF)),
                  _const_spec((D_FF, D_MODEL)), vec(D_MODEL), vec(D_MODEL)],
        out_specs=[row(D_MODEL), row(D_MODEL)],
        compiler_params=_cparams("parallel"),
        name="ffn_ln2",
    )(x, x_bf, wp['w_ffn_in'][l], wp['w_ffn_out'][l], wp['ln2_g'][l], wp['ln2_b'][l])


def _block_diag(blocks):
    n, bi, bj = blocks.shape
    eye = jnp.eye(n, dtype=blocks.dtype)
    return (eye[:, None, :, None] * blocks[:, :, None, :]).reshape(n * bi, n * bj)


def _prepare_weights(P):
    rowvec = lambda a: a.reshape(a.shape[0], 1, -1).astype(F32)
    w_in = P['w_in']
    c0, c1, c2, c3 = RW_COLS, RW_COLS + D_LRU, RW_COLS + 2 * D_LRU, RW_COLS + 2 * D_LRU + D_XA
    zeros_wa = jnp.zeros((DEPTH, W_LORA, D_RW), F32)
    head_red = (jnp.arange(D_RW)[:, None] // RW_HEAD == jnp.arange(LANES)[None, :]).astype(F32)
    wp = {
        'w_rw': w_in[:, :, :c0].astype(BF16),
        'w_lru': w_in[:, :, c0:c2].astype(BF16),
        'w_q': w_in[:, :, c2:c3].astype(BF16),
        'w_gate': w_in[:, :, c3:].astype(BF16),
        'w_lora': jnp.concatenate(
            [jnp.concatenate([P['rw_w2'], zeros_wa], axis=2),
             jnp.concatenate([zeros_wa, P['rw_a2']], axis=2)], axis=1).astype(BF16),
        'rw_g2': P['rw_g2'].astype(BF16),
        'rw_v1': jnp.pad(P['rw_v1'], ((0, 0), (0, 0), (0, LANES - V_LORA))).astype(BF16),
        'rw_v2': jnp.pad(P['rw_v2'], ((0, 0), (0, LANES - V_LORA), (0, 0))).astype(BF16),
        'rw_rk': rowvec(P['rw_rk'].reshape(DEPTH, D_RW)),
        'head_red': head_red.astype(BF16),
        'head_exp': head_red.T.astype(BF16),
        'w_lru_gate': jnp.concatenate(
            [jax.vmap(_block_diag)(P['lru_w_rg']), jax.vmap(_block_diag)(P['lru_w_ig'])],
            axis=2).astype(BF16),
        'lru_conv_w': P['lru_conv_w'].astype(F32),
        'w_mem_kv': P['w_mem_kv'].astype(BF16),
    }
    for name in ('w_rw_out', 'w_lru_out', 'w_xa_out', 'w_o', 'w_ffn_in', 'w_ffn_out'):
        wp[name] = P[name].astype(BF16)
    for name in ('rw_mu', 'rw_w0', 'rw_a0', 'rw_v0', 'rw_kk', 'rw_ka', 'rw_gn_g', 'rw_gn_b',
                 'lru_conv_b', 'lru_b_rg', 'lru_b_ig', 'lru_lambda', 'ln1_g', 'ln1_b', 'ln2_g', 'ln2_b'):
        wp[name] = rowvec(P[name])
    return wp


def _state_to_scan_layout(s):
    return s.transpose(1, 3, 2, 0)


def _state_from_scan_layout(s):
    return s.transpose(3, 0, 2, 1)


def _trunk(x3, mem_k, mem_v, shift, wkv, conv, h, wp, time_major):
    batch, t_len, _ = x3.shape
    assert t_len >= CONV_W - 1
    n = batch * t_len
    hist_steps = CONV_W - 1
    if time_major:
        groups, G, tiles, tm = 1, batch, 1, n
        x = x3.transpose(1, 0, 2).reshape(n, D_MODEL)
        tm_rows = min(n, ROW_TILE)
    else:
        tm = min(t_len, ROW_TILE)
        groups, G, tiles = batch, 1, t_len // tm
        x = x3.reshape(n, D_MODEL)
        tm_rows = tm
    geom = (groups, G, tiles, tm)
    x_bf = x.astype(BF16)
    v_first = None
    n_sh, n_wkv, n_conv, n_h = [], [], [], []
    for l in range(DEPTH):
        if time_major:
            prev = shift[l][None]
            conv0 = conv[l].transpose(1, 0, 2).reshape(1, hist_steps * batch, D_LRU)
            h0 = h[l][None]
        else:
            prev = jnp.pad(shift[l][:, None, :], ((0, 0), (SUBLANES - 1, 0), (0, 0)))
            conv0 = jnp.pad(conv[l], ((0, 0), (SUBLANES - hist_steps, 0), (0, 0)))
            h0 = h[l][:, None, :]
        prep = _rw_prep(x_bf, prev, wp, l, geom, v_first, log_decay=not time_major)
        scan_ops, (g, bonus, sh_tail, q) = prep[:6], prep[6:10]
        if l == 0:
            v_first = prep[10] if time_major else scan_ops[3]
        if time_major:
            y_s, s_new = _wkv_scan(scan_ops, _state_to_scan_layout(wkv[l]))
            y = y_s.transpose(0, 2, 1).reshape(n, D_RW)
            n_wkv.append(_state_from_scan_layout(s_new))
        else:
            y, s_new = _wkv_chunked(scan_ops, _state_to_pairs(wkv[l]), batch, t_len)
            n_wkv.append(_state_from_pairs(s_new))
        n_sh.append(sh_tail[0] if time_major else sh_tail[:, -1, :])
        hg, conv_tail, h_last = _lru(x_bf, conv0, h0, wp, l, geom)
        if time_major:
            n_conv.append(conv_tail[0].reshape(hist_steps, batch, D_LRU).transpose(1, 0, 2))
            n_h.append(h_last[0])
        else:
            n_conv.append(conv_tail[:, SUBLANES - hist_steps:, :])
            n_h.append(h_last[:, 0, :])
        if time_major:
            q3 = q.reshape(t_len, batch, D_XA).transpose(1, 0, 2)
            t_pad = -t_len % (2 * SUBLANES)
            q3 = jnp.pad(q3, ((0, 0), (0, t_pad), (0, 0)))
            o3 = _attention_cache(q3, mem_k, mem_v, l, ATTN_CACHE_BATCH)
            oatt = o3[:, :t_len].transpose(1, 0, 2).reshape(n, D_XA)
        else:
            q3 = q.reshape(batch, t_len, D_XA)
            oatt = _attention_packed(q3, mem_k[l], min(t_len, 1024)).reshape(n, D_XA)
        x, x_bf = _merge(x, x_bf, y, bonus, g, hg, oatt, wp, l, min(n, MERGE_ROW_TILE))
        x, x_bf = _ffn(x, x_bf, wp, l, min(n, FFN_ROW_TILE))
    if time_major:
        y3 = x.reshape(t_len, batch, D_MODEL).transpose(1, 0, 2)
    else:
        y3 = x.reshape(batch, t_len, D_MODEL)
    return y3, jnp.stack(n_sh), jnp.stack(n_wkv), jnp.stack(n_conv), jnp.stack(n_h)


def kernel(x_prompt, x_sample, mem_prompt, state_rwkv_shift, state_rwkv_wkv, state_lru_conv,
           state_lru_h, cache_mem_k, cache_mem_v, w_in, rw_mu, rw_w0, rw_w2, rw_a0, rw_a2,
           rw_g2, rw_v0, rw_v1, rw_v2, rw_kk, rw_ka, rw_rk, rw_gn_g, rw_gn_b, w_rw_out,
           lru_conv_w, lru_conv_b, lru_w_rg, lru_b_rg, lru_w_ig, lru_b_ig, lru_lambda, w_lru_out,
           w_mem_kv, w_xa_out, w_o, ln1_g, ln1_b, w_ffn_in, w_ffn_out, ln2_g, ln2_b):
    P = dict(w_in=w_in, rw_mu=rw_mu, rw_w0=rw_w0, rw_w2=rw_w2, rw_a0=rw_a0, rw_a2=rw_a2,
             rw_g2=rw_g2, rw_v0=rw_v0, rw_v1=rw_v1, rw_v2=rw_v2, rw_kk=rw_kk, rw_ka=rw_ka,
             rw_rk=rw_rk, rw_gn_g=rw_gn_g, rw_gn_b=rw_gn_b, w_rw_out=w_rw_out,
             lru_conv_w=lru_conv_w, lru_conv_b=lru_conv_b, lru_w_rg=lru_w_rg, lru_b_rg=lru_b_rg,
             lru_w_ig=lru_w_ig, lru_b_ig=lru_b_ig, lru_lambda=lru_lambda, w_lru_out=w_lru_out,
             w_mem_kv=w_mem_kv, w_xa_out=w_xa_out, w_o=w_o, ln1_g=ln1_g, ln1_b=ln1_b,
             w_ffn_in=w_ffn_in, w_ffn_out=w_ffn_out, ln2_g=ln2_g, ln2_b=ln2_b)
    wp = _prepare_weights(P)

    bp, n_mem, _ = mem_prompt.shape
    mem_bf = mem_prompt.reshape(bp * n_mem, D_MODEL).astype(BF16)
    kv = jnp.stack([_matmul(mem_bf, wp['w_mem_kv'][l], F32, "mem_kv").reshape(bp, n_mem, 2 * D_XA)
                    for l in range(DEPTH)])
    new_mem_k = kv[..., :D_XA].reshape(DEPTH, bp, n_mem, XA_HEADS, XA_HEAD)
    new_mem_v = kv[..., D_XA:].reshape(DEPTH, bp, n_mem, XA_HEADS, XA_HEAD)
    dt = x_prompt.dtype
    z_shift = jnp.zeros((DEPTH, bp, RW_COLS), dt)
    z_wkv = jnp.zeros((DEPTH, bp, RW_HEADS, RW_HEAD, RW_HEAD), dt)
    z_conv = jnp.zeros((DEPTH, bp, CONV_W - 1, D_LRU), dt)
    z_h = jnp.zeros((DEPTH, bp, D_LRU), dt)
    y_prompt, sh_p, wkv_p, conv_p, h_p = _trunk(
        x_prompt, kv, None, z_shift, z_wkv, z_conv, z_h, wp, time_major=False)

    y_sample, sh_s, wkv_s, conv_s, h_s = _trunk(
        x_sample, cache_mem_k, cache_mem_v, state_rwkv_shift, state_rwkv_wkv, state_lru_conv,
        state_lru_h, wp, time_major=True)
    return (y_prompt, y_sample, sh_p, wkv_p, conv_p, h_p, new_mem_k, new_mem_v,
            sh_s, wkv_s, conv_s, h_s)
```
